```python
import math
import jax, jax.numpy as jnp
from jax import lax
import numpy as np

D_MODEL = 4096
BATCH = 2
SEQ = 8192
DEPTH = 4

N_MIXERS = 3
GDN_HEAD_DIM = 128
GDN_HEADS = D_MODEL // GDN_HEAD_DIM
GDN_WIDTH = GDN_HEADS * GDN_HEAD_DIM
GDN_CHUNK = 64
GDN_CONV = 4
SB_HEAD_DIM = 128
SB_HEADS = D_MODEL // SB_HEAD_DIM
SB_WIDTH = SB_HEADS * SB_HEAD_DIM
SB_BLOCK = 128
LRU_WIDTH = D_MODEL
LRU_BLOCK_DIM = 256
LRU_BLOCKS = LRU_WIDTH // LRU_BLOCK_DIM
LRU_CONV = 4
LRU_C = 8.0
D_FF = 256 * ((8 * D_MODEL // 3 + 255) // 256)
FFN_CONV = 3
DEEPNORM_ALPHA = (2 * DEPTH) ** 0.25
DEEPNORM_BETA = (8 * DEPTH) ** -0.25
N_GDN_LAYERS = (DEPTH + 2) // 3
N_SB_LAYERS = (DEPTH + 1) // 3
N_LRU_LAYERS = DEPTH // 3
LN_EPS = 1e-5
RMS_EPS = 1e-6

kernel_name = 'hybrid_gdn_stickbreaking_rglru_convffn'


def layer_norm(x, gain, bias):
    xf = x.astype(jnp.float32)
    mu = jnp.mean(xf, axis=-1, keepdims=True)
    var = jnp.mean(jnp.square(xf - mu), axis=-1, keepdims=True)
    y = (xf - mu) * lax.rsqrt(var + LN_EPS) * gain.astype(jnp.float32) + bias.astype(jnp.float32)
    return y.astype(x.dtype)


def l2_normalize(x):
    return x * lax.rsqrt(jnp.sum(jnp.square(x), axis=-1, keepdims=True) + RMS_EPS)


def causal_dwconv(x, w):
    K, C = w.shape
    return lax.conv_general_dilated(x, w[:, None, :].astype(x.dtype), window_strides=(1,), padding=[(K - 1, 0)], dimension_numbers=('NWC', 'WIO', 'NWC'), feature_group_count=C)


def chunk_gated_delta_rule(q, k, v, g, beta):
    B, H, T, DK = q.shape
    DV = v.shape[-1]
    C = GDN_CHUNK
    N = T // C
    q = l2_normalize(q) * DK ** -0.5
    k = l2_normalize(k)
    q, k, v = (t.reshape(B, H, N, C, t.shape[-1]) for t in (q, k, v))
    g = jnp.cumsum(g.reshape(B, H, N, C), axis=-1)
    beta = beta.reshape(B, H, N, C)
    idx = jnp.arange(C)
    incl = idx[:, None] >= idx[None, :]
    strict = idx[:, None] > idx[None, :]
    gdiff = g[..., :, None] - g[..., None, :]
    decay = jnp.where(incl, jnp.exp(jnp.where(incl, gdiff, 0.0)), 0.0)
    k_beta = k * beta[..., None]
    kk = jnp.einsum('bhncd,bhnsd->bhncs', k_beta, k) * decay
    tri = jnp.eye(C, dtype=q.dtype) + jnp.where(strict, kk, 0.0)
    rhs = jnp.concatenate([v * beta[..., None], k_beta * jnp.exp(g)[..., None]], axis=-1)
    sol = lax.linalg.triangular_solve(tri, rhs, left_side=True, lower=True, unit_diagonal=True)
    u, w = sol[..., :DV], sol[..., DV:]
    qk = jnp.einsum('bhncd,bhnsd->bhncs', q, k) * decay
    g_last = g[..., -1]
    q_dec = q * jnp.exp(g)[..., None]
    k_tail = k * jnp.exp(g_last[..., None] - g)[..., None]
    chunk_decay = jnp.exp(g_last)
    xs = tuple(jnp.moveaxis(t, 2, 0) for t in (q_dec, k_tail, u, w, qk, chunk_decay))

    def step(S, inp):
        q_c, k_c, u_c, w_c, qk_c, dec_c = inp
        v_new = u_c - jnp.einsum('bhcd,bhdv->bhcv', w_c, S)
        o = jnp.einsum('bhcd,bhdv->bhcv', q_c, S) + jnp.einsum('bhcs,bhsv->bhcv', qk_c, v_new)
        S = S * dec_c[..., None, None] + jnp.einsum('bhcd,bhcv->bhdv', k_c, v_new)
        return S, o

    _, o = lax.scan(step, jnp.zeros((B, H, DK, DV), jnp.float32), xs)
    return jnp.moveaxis(o, 0, 2).reshape(B, H, T, DV)


def gated_deltanet_mixer(x, w_in, conv_w, a_log, dt_bias, norm_w, w_out):
    B, T, _ = x.shape
    proj = x @ w_in
    qkv, z, b, a = jnp.split(proj, [3 * GDN_WIDTH, 4 * GDN_WIDTH, 4 * GDN_WIDTH + GDN_HEADS], axis=-1)
    qkv = jax.nn.silu(causal_dwconv(qkv, conv_w))
    q, k, v = (t.reshape(B, T, GDN_HEADS, GDN_HEAD_DIM).transpose(0, 2, 1, 3).astype(jnp.float32) for t in jnp.split(qkv, 3, axis=-1))
    beta = jax.nn.sigmoid(b.astype(jnp.float32)).transpose(0, 2, 1)
    g = -(jnp.exp(a_log.astype(jnp.float32)) * jax.nn.softplus(a.astype(jnp.float32) + dt_bias.astype(jnp.float32))).transpose(0, 2, 1)
    o = chunk_gated_delta_rule(q, k, v, g, beta).transpose(0, 2, 1, 3)
    o = o * lax.rsqrt(jnp.mean(jnp.square(o), axis=-1, keepdims=True) + RMS_EPS) * norm_w.astype(jnp.float32)
    o = o * jax.nn.silu(z.reshape(B, T, GDN_HEADS, GDN_HEAD_DIM).astype(jnp.float32))
    return o.reshape(B, T, GDN_WIDTH).astype(x.dtype) @ w_out


def stick_breaking_mixer(x, w_in, w_out):
    B, T, _ = x.shape
    q, k, v = (t.reshape(B, T, SB_HEADS, SB_HEAD_DIM).transpose(0, 2, 1, 3) for t in jnp.split(x @ w_in, 3, axis=-1))
    n_blocks = T // SB_BLOCK
    scale = SB_HEAD_DIM ** -0.5
    k_pos = jnp.arange(T)
    q_blocks = jnp.moveaxis(q.reshape(B, SB_HEADS, n_blocks, SB_BLOCK, SB_HEAD_DIM), 2, 0)

    def attend_block(args):
        q_blk, blk = args
        q_pos = blk * SB_BLOCK + jnp.arange(SB_BLOCK)
        before = k_pos[None, :] < q_pos[:, None]
        z = jnp.einsum('bhqd,bhkd->bhqk', q_blk, k, preferred_element_type=jnp.float32) * scale
        log_beta = jax.nn.log_sigmoid(z)
        log_keep = jnp.where(before, jax.nn.log_sigmoid(-z), 0.0)
        log_keep_after = lax.cumsum(log_keep, axis=3, reverse=True) - log_keep
        weight = jnp.where(before, jnp.exp(log_beta + log_keep_after), 0.0)
        return jnp.einsum('bhqk,bhkd->bhqd', weight.astype(v.dtype), v)

    o = lax.map(attend_block, (q_blocks, jnp.arange(n_blocks)))
    o = jnp.moveaxis(o, 0, 2).reshape(B, SB_HEADS, T, SB_HEAD_DIM).transpose(0, 2, 1, 3).reshape(B, T, SB_WIDTH)
    return o @ w_out


def rglru_mixer(x, w_in, conv_w, conv_b, w_gate_a, b_gate_a, w_gate_x, b_gate_x, lam, w_out):
    B, T, _ = x.shape
    xr, y = jnp.split(x @ w_in, 2, axis=-1)
    y = jax.nn.gelu(y)
    xr = causal_dwconv(xr, conv_w) + conv_b
    xf = xr.astype(jnp.float32)
    xb = xf.reshape(B, T, LRU_BLOCKS, LRU_BLOCK_DIM)
    r = jax.nn.sigmoid(jnp.einsum('btni,nij->btnj', xb, w_gate_a.astype(jnp.float32)) + b_gate_a.astype(jnp.float32)).reshape(B, T, LRU_WIDTH)
    i_gate = jax.nn.sigmoid(jnp.einsum('btni,nij->btnj', xb, w_gate_x.astype(jnp.float32)) + b_gate_x.astype(jnp.float32)).reshape(B, T, LRU_WIDTH)
    log_a = -LRU_C * r * jax.nn.softplus(-lam.astype(jnp.float32))
    a = jnp.exp(log_a)
    u = jnp.sqrt(-jnp.expm1(2.0 * log_a)) * i_gate * xf

    def step(h, au):
        a_t, u_t = au
        h = a_t * h + u_t
        return h, h

    _, h = lax.scan(step, jnp.zeros((B, LRU_WIDTH), jnp.float32), (jnp.swapaxes(a, 0, 1), jnp.swapaxes(u, 0, 1)))
    h = jnp.swapaxes(h, 0, 1).astype(x.dtype)
    return (h * y) @ w_out


def conv_ffn(x, w_up, conv_w, conv_b, w_down):
    gate, up = jnp.split(x @ w_up, 2, axis=-1)
    gate = causal_dwconv(gate, conv_w) + conv_b
    return (jax.nn.silu(gate) * up) @ w_down


def setup_inputs(seed: int = 0) -> dict:
    key = jax.random.key(seed)
    ks = iter(jax.random.split(key, 40))

    def nrm(shape, scale):
        return scale * jax.random.normal(next(ks), shape, jnp.float32)

    nA, nB, nC = N_GDN_LAYERS, N_SB_LAYERS, N_LRU_LAYERS
    x = nrm((BATCH, SEQ, D_MODEL), 1.0)
    ln_gain = 1.0 + nrm((DEPTH, 2, D_MODEL), 0.02)
    ln_bias = nrm((DEPTH, 2, D_MODEL), 0.02)
    gdn_w_in = nrm((nA, D_MODEL, 4 * GDN_WIDTH + 2 * GDN_HEADS), D_MODEL ** -0.5)
    gdn_conv_w = nrm((nA, GDN_CONV, 3 * GDN_WIDTH), GDN_CONV ** -0.5)
    gdn_a_log = jnp.log(jax.random.uniform(next(ks), (nA, GDN_HEADS), jnp.float32, 1.0, 16.0))
    dt = jnp.exp(jax.random.uniform(next(ks), (nA, GDN_HEADS), jnp.float32, math.log(1e-3), math.log(1e-1)))
    gdn_dt_bias = dt + jnp.log(-jnp.expm1(-dt))
    gdn_norm_w = 1.0 + nrm((nA, GDN_HEAD_DIM), 0.02)
    gdn_w_out = nrm((nA, GDN_WIDTH, D_MODEL), GDN_WIDTH ** -0.5 * DEEPNORM_BETA)
    sb_w_in = nrm((nB, D_MODEL, 3 * SB_WIDTH), D_MODEL ** -0.5)
    sb_w_out = nrm((nB, SB_WIDTH, D_MODEL), SB_WIDTH ** -0.5 * DEEPNORM_BETA)
    lru_w_in = nrm((nC, D_MODEL, 2 * LRU_WIDTH), D_MODEL ** -0.5)
    lru_conv_w = nrm((nC, LRU_CONV, LRU_WIDTH), LRU_CONV ** -0.5)
    lru_conv_b = nrm((nC, LRU_WIDTH), 0.01)
    lru_w_gate_a = nrm((nC, LRU_BLOCKS, LRU_BLOCK_DIM, LRU_BLOCK_DIM), LRU_BLOCK_DIM ** -0.5)
    lru_b_gate_a = nrm((nC, LRU_BLOCKS, LRU_BLOCK_DIM), 0.01)
    lru_w_gate_x = nrm((nC, LRU_BLOCKS, LRU_BLOCK_DIM, LRU_BLOCK_DIM), LRU_BLOCK_DIM ** -0.5)
    lru_b_gate_x = nrm((nC, LRU_BLOCKS, LRU_BLOCK_DIM), 0.01)
    a_c = jax.random.uniform(next(ks), (nC, LRU_WIDTH), jnp.float32, 0.9, 0.999)
    s = a_c ** (1.0 / LRU_C)
    lru_lambda = jnp.log(s) - jnp.log1p(-s)
    lru_w_out = nrm((nC, LRU_WIDTH, D_MODEL), LRU_WIDTH ** -0.5 * DEEPNORM_BETA)
    ffn_w_up = nrm((DEPTH, D_MODEL, 2 * D_FF), D_MODEL ** -0.5)
    ffn_conv_w = nrm((DEPTH, FFN_CONV, D_FF), FFN_CONV ** -0.5)
    ffn_conv_b = nrm((DEPTH, D_FF), 0.01)
    ffn_w_down = nrm((DEPTH, D_FF, D_MODEL), D_FF ** -0.5 * DEEPNORM_BETA)
    return {'x': x, 'ln_gain': ln_gain, 'ln_bias': ln_bias,
            'gdn_w_in': gdn_w_in, 'gdn_conv_w': gdn_conv_w, 'gdn_a_log': gdn_a_log, 'gdn_dt_bias': gdn_dt_bias,
            'gdn_norm_w': gdn_norm_w, 'gdn_w_out': gdn_w_out,
            'sb_w_in': sb_w_in, 'sb_w_out': sb_w_out,
            'lru_w_in': lru_w_in, 'lru_conv_w': lru_conv_w, 'lru_conv_b': lru_conv_b,
            'lru_w_gate_a': lru_w_gate_a, 'lru_b_gate_a': lru_b_gate_a, 'lru_w_gate_x': lru_w_gate_x,
            'lru_b_gate_x': lru_b_gate_x, 'lru_lambda': lru_lambda, 'lru_w_out': lru_w_out,
            'ffn_w_up': ffn_w_up, 'ffn_conv_w': ffn_conv_w, 'ffn_conv_b': ffn_conv_b, 'ffn_w_down': ffn_w_down}


def reference(x, ln_gain, ln_bias,
              gdn_w_in, gdn_conv_w, gdn_a_log, gdn_dt_bias, gdn_norm_w, gdn_w_out,
              sb_w_in, sb_w_out,
              lru_w_in, lru_conv_w, lru_conv_b, lru_w_gate_a, lru_b_gate_a, lru_w_gate_x, lru_b_gate_x,
              lru_lambda, lru_w_out,
              ffn_w_up, ffn_conv_w, ffn_conv_b, ffn_w_down):
    for i in range(DEPTH):
        kind, j = i % N_MIXERS, i // N_MIXERS
        if kind == 0:
            mix = gated_deltanet_mixer(x, gdn_w_in[j], gdn_conv_w[j], gdn_a_log[j], gdn_dt_bias[j], gdn_norm_w[j], gdn_w_out[j])
        elif kind == 1:
            mix = stick_breaking_mixer(x, sb_w_in[j], sb_w_out[j])
        else:
            mix = rglru_mixer(x, lru_w_in[j], lru_conv_w[j], lru_conv_b[j], lru_w_gate_a[j], lru_b_gate_a[j],
                              lru_w_gate_x[j], lru_b_gate_x[j], lru_lambda[j], lru_w_out[j])
        x = layer_norm(DEEPNORM_ALPHA * x + mix, ln_gain[i, 0], ln_bias[i, 0])
        x = layer_norm(DEEPNORM_ALPHA * x + conv_ffn(x, ffn_w_up[i], ffn_conv_w[i], ffn_conv_b[i], ffn_w_down[i]), ln_gain[i, 1], ln_bias[i, 1])
    return x
```

```python
import functools
import math

import jax
import jax.numpy as jnp
from jax import lax
from jax.experimental import pallas as pl
from jax.experimental.pallas import tpu as pltpu

F32 = jnp.float32
BF16 = jnp.bfloat16

LN_EPS = 1e-5
RMS_EPS = 1e-6
HEAD_DIM = 128
GDN_CHUNK = 64
LRU_BLOCK_DIM = 256
LRU_C = 8.0
SUBLANES = 8
LANES = 128
VMEM_LIMIT_BYTES = 56 * 1024 * 1024


def _pick(n, candidates):
    for c in candidates:
        if n % c == 0:
            return c
    raise ValueError(f"no tile in {candidates} divides {n}")


def _params(n_axes):
    return pltpu.CompilerParams(dimension_semantics=("arbitrary",) * n_axes,
                                vmem_limit_bytes=VMEM_LIMIT_BYTES)


def _softplus(x):
    return jnp.maximum(x, 0.0) + jnp.log1p(jnp.exp(-jnp.abs(x)))


def _dot(a, b):
    return jnp.dot(a, b, preferred_element_type=F32)


def _dot_nt(a, b):
    return lax.dot_general(a, b, (((1,), (1,)), ((), ())), preferred_element_type=F32)


def _proj_kernel(*refs, taps, has_bias, act, gated, tm, tiles_per_seq):
    it = iter(refs)
    x_ref = next(it)
    w_ref = next(it)
    w2_ref = next(it) if gated else None
    cw_ref = next(it) if taps else None
    b_ref = next(it) if has_bias else None
    o_ref = next(it)
    ext_ref = next(it) if taps else None
    halo_ref = next(it) if taps else None

    i = pl.program_id(0)
    j = pl.program_id(1)
    x = x_ref[...]
    acc = _dot(x, w_ref[...])
    if taps:
        seq_start = (i % tiles_per_seq) == 0

        @pl.when(seq_start)
        def _():
            ext_ref[0:SUBLANES, :] = jnp.zeros((SUBLANES, acc.shape[1]), F32)

        @pl.when(jnp.logical_not(seq_start))
        def _():
            ext_ref[0:SUBLANES, :] = halo_ref[j]

        ext_ref[SUBLANES:, :] = acc
        halo_ref[j] = acc[tm - SUBLANES:, :]
        out = acc * cw_ref[taps - 1:taps, :]
        for s in range(1, taps):
            out = out + ext_ref[SUBLANES - s:SUBLANES - s + tm, :] * cw_ref[taps - 1 - s:taps - s, :]
    else:
        out = acc
    if has_bias:
        out = out + b_ref[...]
    if act == "silu":
        out = out * jax.nn.sigmoid(out)
    elif act == "gelu":
        out = jax.nn.gelu(out, approximate=True)
    if gated:
        out = out * _dot(x, w2_ref[...])
    o_ref[...] = out.astype(o_ref.dtype)


def _proj(xb, w, *, col0, ncols, seq_len, out_dtype, cw=None, bias=None, act="none", gate_col0=None):
    m, k = xb.shape
    gated = gate_col0 is not None
    tn = _pick(ncols, (256,) if gated else (512, 256, 128))
    tm = _pick(seq_len, (1024, 512, 256, 128))
    taps = 0 if cw is None else cw.shape[0]
    nj = ncols // tn
    c0 = col0 // tn
    assert col0 % tn == 0 and m % seq_len == 0
    in_specs = [pl.BlockSpec((tm, k), lambda i, j: (i, 0)),
                pl.BlockSpec((k, tn), lambda i, j: (0, c0 + j))]
    args = [xb, w]
    if gated:
        assert gate_col0 % tn == 0
        g0 = gate_col0 // tn
        in_specs.append(pl.BlockSpec((k, tn), lambda i, j: (0, g0 + j)))
        args.append(w)
    scratch = []
    if taps:
        in_specs.append(pl.BlockSpec((taps, tn), lambda i, j: (0, j)))
        args.append(cw)
        scratch = [pltpu.VMEM((tm + SUBLANES, tn), F32), pltpu.VMEM((nj, SUBLANES, tn), F32)]
    if bias is not None:
        in_specs.append(pl.BlockSpec((1, tn), lambda i, j: (0, j)))
        args.append(bias.reshape(1, ncols))
    kern = functools.partial(_proj_kernel, taps=taps, has_bias=bias is not None, act=act, gated=gated,
                             tm=tm, tiles_per_seq=seq_len // tm)
    return pl.pallas_call(
        kern,
        grid=(m // tm, nj),
        in_specs=in_specs,
        out_specs=pl.BlockSpec((tm, tn), lambda i, j: (i, j)),
        out_shape=jax.ShapeDtypeStruct((m, ncols), out_dtype),
        scratch_shapes=scratch,
        compiler_params=_params(2),
    )(*args)


def _outproj_ln_kernel(a_ref, w_ref, x_ref, g_ref, b_ref, of_ref, ob_ref, *, alpha, nk):
    kk = pl.program_id(1)
    prod = _dot(a_ref[...], w_ref[...])

    @pl.when(kk == 0)
    def _():
        of_ref[...] = prod

    @pl.when(kk > 0)
    def _():
        of_ref[...] += prod

    @pl.when(kk == nk - 1)
    def _():
        y = alpha * x_ref[...] + of_ref[...]
        mu = jnp.mean(y, axis=-1, keepdims=True)
        yc = y - mu
        var = jnp.mean(yc * yc, axis=-1, keepdims=True)
        out = yc * lax.rsqrt(var + LN_EPS) * g_ref[...] + b_ref[...]
        of_ref[...] = out
        ob_ref[...] = out.astype(BF16)


def _outproj_ln(a, w, x, gain, bias, alpha):
    m, k = a.shape
    d = w.shape[1]
    tm = _pick(m, (256, 128))
    tk = _pick(k, (512, 256, 128))
    nk = k // tk
    kern = functools.partial(_outproj_ln_kernel, alpha=alpha, nk=nk)
    return pl.pallas_call(
        kern,
        grid=(m // tm, nk),
        in_specs=[pl.BlockSpec((tm, tk), lambda i, kk: (i, kk)),
                  pl.BlockSpec((tk, d), lambda i, kk: (kk, 0)),
                  pl.BlockSpec((tm, d), lambda i, kk: (i, 0)),
                  pl.BlockSpec((1, d), lambda i, kk: (0, 0)),
                  pl.BlockSpec((1, d), lambda i, kk: (0, 0))],
        out_specs=[pl.BlockSpec((tm, d), lambda i, kk: (i, 0)),
                   pl.BlockSpec((tm, d), lambda i, kk: (i, 0))],
        out_shape=[jax.ShapeDtypeStruct((m, d), F32), jax.ShapeDtypeStruct((m, d), BF16)],
        compiler_params=_params(2),
    )(a, w, x, gain.reshape(1, d), bias.reshape(1, d))


def _gdn_gates_kernel(x_ref, w_ref, alog_ref, dtb_ref, gates_ref, gsuf_ref, *, heads, chunk):
    p = _dot(x_ref[...], w_ref[...])
    tm = p.shape[0]
    lane = lax.broadcasted_iota(jnp.int32, p.shape, 1)
    pos = lax.broadcasted_iota(jnp.int32, p.shape, 0) % chunk
    g = -jnp.exp(alog_ref[...]) * _softplus(p + dtb_ref[...])
    pre = g
    suf = g
    s = 1
    while s < chunk:
        pre = pre + jnp.where(pos >= s, pltpu.roll(pre, s, 0), 0.0)
        suf = suf + jnp.where(pos < chunk - s, pltpu.roll(suf, tm - s, 0), 0.0)
        s *= 2
    gates_ref[...] = jnp.where(lane < heads, jax.nn.sigmoid(p), pre)
    gsuf_ref[...] = suf - g


def _gdn_gates(xb, w_ba, a_log, dt_bias, heads):
    m, k = xb.shape
    tm = _pick(m, (512, 256, 128))
    pad = LANES - 2 * heads
    wp = jnp.pad(w_ba, ((0, 0), (0, pad)))
    alog = jnp.pad(a_log.reshape(1, heads), ((0, 0), (heads, pad)))
    dtb = jnp.pad(dt_bias.reshape(1, heads), ((0, 0), (heads, pad)))
    kern = functools.partial(_gdn_gates_kernel, heads=heads, chunk=GDN_CHUNK)
    return pl.pallas_call(
        kern,
        grid=(m // tm,),
        in_specs=[pl.BlockSpec((tm, k), lambda i: (i, 0)),
                  pl.BlockSpec((k, LANES), lambda i: (0, 0)),
                  pl.BlockSpec((1, LANES), lambda i: (0, 0)),
                  pl.BlockSpec((1, LANES), lambda i: (0, 0))],
        out_specs=[pl.BlockSpec((tm, LANES), lambda i: (i, 0)),
                   pl.BlockSpec((tm, LANES), lambda i: (i, 0))],
        out_shape=[jax.ShapeDtypeStruct((m, LANES), F32), jax.ShapeDtypeStruct((m, LANES), F32)],
        compiler_params=_params(1),
    )(xb, wp, alog, dtb)


def _gdn_kernel(q_ref, k_ref, v_ref, z_ref, gates_ref, gsuf_ref, gcrow_ref, nw_ref, o_ref, s_ref, *,
                heads, chunk):
    h = pl.program_id(1)
    t = pl.program_id(2)

    @pl.when(t == 0)
    def _():
        s_ref[...] = jnp.zeros(s_ref.shape, F32)

    q = q_ref[...]
    k = k_ref[...]
    v = v_ref[...]
    tc, dk = q.shape
    lane = lax.broadcasted_iota(jnp.int32, (tc, LANES), 1)
    gts = gates_ref[...]
    beta = jnp.sum(jnp.where(lane == h, gts, 0.0), axis=1, keepdims=True)
    gc = jnp.sum(jnp.where(lane == heads + h, gts, 0.0), axis=1, keepdims=True)
    gs = jnp.sum(jnp.where(lane == heads + h, gsuf_ref[...], 0.0), axis=1, keepdims=True)
    gc_row = gcrow_ref[...]

    qn = q * lax.rsqrt(jnp.sum(q * q, axis=-1, keepdims=True) + RMS_EPS) * (dk ** -0.5)
    kn = k * lax.rsqrt(jnp.sum(k * k, axis=-1, keepdims=True) + RMS_EPS)
    kb = kn * beta

    shift = int(math.log2(chunk))
    rows = lax.broadcasted_iota(jnp.int32, (tc, tc), 0)
    cols = lax.broadcasted_iota(jnp.int32, (tc, tc), 1)
    same = (rows >> shift) == (cols >> shift)
    incl = same & (rows >= cols)
    strict = same & (rows > cols)
    decay = jnp.where(incl, jnp.exp(jnp.where(incl, gc - gc_row, 0.0)), 0.0)

    kn16 = kn.astype(BF16)
    a = jnp.where(strict, _dot_nt(kb.astype(BF16), kn16) * decay, 0.0)
    qk = _dot_nt(qn.astype(BF16), kn16) * decay

    p = jnp.where(rows == cols, 1.0, 0.0) - a
    ak = a
    for _ in range(shift - 1):
        ak16 = ak.astype(BF16)
        ak = _dot(ak16, ak16)
        p = p + _dot(p.astype(BF16), ak.astype(BF16))

    egc = jnp.exp(gc)
    rhs = jnp.concatenate([v * beta, kb * egc], axis=1)
    sol = _dot(p.astype(BF16), rhs.astype(BF16))
    dv = v.shape[1]
    u = sol[:, :dv]
    w = sol[:, dv:]
    q_dec = qn * egc
    k_tail = kn * jnp.exp(gs)

    state = s_ref[...]
    outs = []
    n_chunks = tc // chunk
    for c in range(n_chunks):
        lo, hi = c * chunk, (c + 1) * chunk
        lhs = jnp.concatenate([w[lo:hi], q_dec[lo:hi]], axis=0).astype(BF16)
        r = _dot(lhs, state.astype(BF16))
        v_new = u[lo:hi] - r[:chunk]
        pieces = []
        if lo:
            pieces.append(jnp.zeros((lo, dv), F32))
        pieces.append(v_new)
        if tc - hi:
            pieces.append(jnp.zeros((tc - hi, dv), F32))
        v_pad = jnp.concatenate(pieces, axis=0) if len(pieces) > 1 else v_new
        outs.append(r[chunk:] + _dot(qk[lo:hi].astype(BF16), v_pad.astype(BF16)))
        state = state * jnp.exp(gc[hi - 1:hi, :]) + _dot(k_tail[lo:hi].T.astype(BF16), v_new.astype(BF16))
    s_ref[...] = state
    o = jnp.concatenate(outs, axis=0)
    o = o * lax.rsqrt(jnp.mean(o * o, axis=-1, keepdims=True) + RMS_EPS) * nw_ref[...]
    zz = z_ref[...]
    o_ref[...] = (o * (zz * jax.nn.sigmoid(zz))).astype(o_ref.dtype)


def _gdn_core(qkv, z, gates, gsuf, norm_w, *, batch, seq_len, heads):
    m = qkv.shape[0]
    tc = _pick(seq_len, (256, 128, 64))
    nt = seq_len // tc
    gc_rows = gates[:, heads:2 * heads].T.reshape(heads, 1, m)
    kern = functools.partial(_gdn_kernel, heads=heads, chunk=GDN_CHUNK)

    def tok(col):
        return pl.BlockSpec((tc, HEAD_DIM), lambda b, h, t: (b * nt + t, col * heads + h))

    return pl.pallas_call(
        kern,
        grid=(batch, heads, nt),
        in_specs=[tok(0), tok(1), tok(2),
                  pl.BlockSpec((tc, HEAD_DIM), lambda b, h, t: (b * nt + t, h)),
                  pl.BlockSpec((tc, LANES), lambda b, h, t: (b * nt + t, 0)),
                  pl.BlockSpec((tc, LANES), lambda b, h, t: (b * nt + t, 0)),
                  pl.BlockSpec((None, 1, tc), lambda b, h, t: (h, 0, b * nt + t)),
                  pl.BlockSpec((1, HEAD_DIM), lambda b, h, t: (0, 0))],
        out_specs=pl.BlockSpec((tc, HEAD_DIM), lambda b, h, t: (b * nt + t, h)),
        out_shape=jax.ShapeDtypeStruct((m, heads * HEAD_DIM), BF16),
        scratch_shapes=[pltpu.VMEM((HEAD_DIM, HEAD_DIM), F32)],
        compiler_params=_params(3),
    )(qkv, qkv, qkv, z, gates, gsuf, gc_rows, norm_w.reshape(1, HEAD_DIM))


def _gdn_mixer(xb, w_in, conv_w, a_log, dt_bias, norm_w, *, batch, seq_len):
    d = xb.shape[1]
    heads = d // HEAD_DIM
    w16 = w_in.astype(BF16)
    qkv = _proj(xb, w16, col0=0, ncols=3 * d, seq_len=seq_len, out_dtype=F32, cw=conv_w, act="silu")
    z = _proj(xb, w16, col0=3 * d, ncols=d, seq_len=seq_len, out_dtype=F32)
    gates, gsuf = _gdn_gates(xb, w16[:, 4 * d:], a_log, dt_bias, heads)
    return _gdn_core(qkv, z, gates, gsuf, norm_w, batch=batch, seq_len=seq_len, heads=heads)


def _sb_kernel(q_ref, k_ref, v_ref, o_ref, *, blk, scale):
    i = pl.program_id(2)
    q = q_ref[...]
    rows = lax.broadcasted_iota(jnp.int32, (blk, blk), 0)
    cols = lax.broadcasted_iota(jnp.int32, (blk, blk), 1)
    after = jnp.where(rows > cols, 1.0, 0.0).astype(BF16)
    before = cols < rows

    def block(j, acc, run, masked):
        start = pl.multiple_of(j * blk, blk)
        kb = k_ref[pl.ds(start, blk), :]
        vb = v_ref[pl.ds(start, blk), :]
        z = _dot_nt(q, kb) * scale
        sp = _softplus(z)
        log_keep = -sp
        if masked:
            log_keep = jnp.where(before, log_keep, 0.0)
        hi = log_keep.astype(BF16)
        lo = (log_keep - hi.astype(F32)).astype(BF16)
        keep_after = _dot(hi, after) + _dot(lo, after) + run
        wgt = jnp.exp(z - sp + keep_after)
        if masked:
            wgt = jnp.where(before, wgt, 0.0)
        acc = acc + _dot(wgt.astype(BF16), vb)
        run = run + jnp.sum(log_keep, axis=1, keepdims=True)
        return acc, run

    acc, run = block(i, jnp.zeros((blk, q.shape[1]), F32), jnp.zeros((blk, 1), F32), True)
    acc, run = lax.fori_loop(0, i, lambda jj, c: block(i - 1 - jj, c[0], c[1], False), (acc, run))
    o_ref[...] = acc.astype(o_ref.dtype)


def _sb_attention(qkv, *, batch, seq_len, heads):
    m = qkv.shape[0]
    blk = _pick(seq_len, (256, 128))
    nq = seq_len // blk
    kern = functools.partial(_sb_kernel, blk=blk, scale=HEAD_DIM ** -0.5)
    return pl.pallas_call(
        kern,
        grid=(batch, heads, nq),
        in_specs=[pl.BlockSpec((blk, HEAD_DIM), lambda b, h, i: (b * nq + i, h)),
                  pl.BlockSpec((seq_len, HEAD_DIM), lambda b, h, i: (b, heads + h)),
                  pl.BlockSpec((seq_len, HEAD_DIM), lambda b, h, i: (b, 2 * heads + h))],
        out_specs=pl.BlockSpec((blk, HEAD_DIM), lambda b, h, i: (b * nq + i, h)),
        out_shape=jax.ShapeDtypeStruct((m, heads * HEAD_DIM), BF16),
        compiler_params=_params(3),
    )(qkv, qkv, qkv)


def _sb_mixer(xb, w_in, *, batch, seq_len):
    d = xb.shape[1]
    qkv = _proj(xb, w_in.astype(BF16), col0=0, ncols=3 * d, seq_len=seq_len, out_dtype=BF16)
    return _sb_attention(qkv, batch=batch, seq_len=seq_len, heads=d // HEAD_DIM)


def _lru_kernel(xr_ref, y_ref, wa_ref, wx_ref, ba_ref, bx_ref, lam_ref, o_ref, h_ref, a_scr, u_scr):
    t = pl.program_id(2)

    @pl.when(t == 0)
    def _():
        h_ref[...] = jnp.zeros(h_ref.shape, F32)

    x = xr_ref[...]
    x16 = x.astype(BF16)
    r = jax.nn.sigmoid(_dot(x16, wa_ref[...]) + ba_ref[...])
    ig = jax.nn.sigmoid(_dot(x16, wx_ref[...]) + bx_ref[...])
    log_a = -LRU_C * r * _softplus(-lam_ref[...])
    a = jnp.exp(log_a)
    a_scr[...] = a
    u_scr[...] = jnp.sqrt(-jnp.tanh(log_a) * (a * a + 1.0)) * ig * x

    tt, c = x.shape
    pos = lax.broadcasted_iota(jnp.int32, (SUBLANES, c), 0)

    def slab(s, h):
        r0 = pl.multiple_of(s * SUBLANES, SUBLANES)
        a8 = a_scr[pl.ds(r0, SUBLANES), :]
        u8 = u_scr[pl.ds(r0, SUBLANES), :]
        d = 1
        while d < SUBLANES:
            a_prev = jnp.where(pos >= d, pltpu.roll(a8, d, 0), 1.0)
            u_prev = jnp.where(pos >= d, pltpu.roll(u8, d, 0), 0.0)
            u8 = a8 * u_prev + u8
            a8 = a8 * a_prev
            d *= 2
        h8 = a8 * h + u8
        u_scr[pl.ds(r0, SUBLANES), :] = h8
        return h8[SUBLANES - 1:SUBLANES, :]

    h_ref[...] = lax.fori_loop(0, tt // SUBLANES, slab, h_ref[...])
    o_ref[...] = (u_scr[...] * y_ref[...]).astype(o_ref.dtype)


def _lru_core(xr, y, w_gate_a, b_gate_a, w_gate_x, b_gate_x, lam, *, batch, seq_len):
    m, d = xr.shape
    nb = d // LRU_BLOCK_DIM
    tt = _pick(seq_len, (512, 256, 128))
    nt = seq_len // tt

    def tok():
        return pl.BlockSpec((tt, LRU_BLOCK_DIM), lambda b, n, t: (b * nt + t, n))

    def per_block(rows):
        return pl.BlockSpec((None, rows, LRU_BLOCK_DIM), lambda b, n, t: (n, 0, 0))

    return pl.pallas_call(
        _lru_kernel,
        grid=(batch, nb, nt),
        in_specs=[tok(), tok(), per_block(LRU_BLOCK_DIM), per_block(LRU_BLOCK_DIM),
                  per_block(1), per_block(1), per_block(1)],
        out_specs=tok(),
        out_shape=jax.ShapeDtypeStruct((m, d), BF16),
        scratch_shapes=[pltpu.VMEM((1, LRU_BLOCK_DIM), F32),
                        pltpu.VMEM((tt, LRU_BLOCK_DIM), F32),
                        pltpu.VMEM((tt, LRU_BLOCK_DIM), F32)],
        compiler_params=_params(3),
    )(xr, y, w_gate_a.astype(BF16), w_gate_x.astype(BF16),
      b_gate_a.reshape(nb, 1, LRU_BLOCK_DIM), b_gate_x.reshape(nb, 1, LRU_BLOCK_DIM),
      lam.reshape(nb, 1, LRU_BLOCK_DIM))


def _lru_mixer(xb, w_in, conv_w, conv_b, w_gate_a, b_gate_a, w_gate_x, b_gate_x, lam, *, batch, seq_len):
    d = xb.shape[1]
    w16 = w_in.astype(BF16)
    xr = _proj(xb, w16, col0=0, ncols=d, seq_len=seq_len, out_dtype=F32, cw=conv_w, bias=conv_b)
    y = _proj(xb, w16, col0=d, ncols=d, seq_len=seq_len, out_dtype=F32, act="gelu")
    return _lru_core(xr, y, w_gate_a, b_gate_a, w_gate_x, b_gate_x, lam, batch=batch, seq_len=seq_len)


def kernel(x, ln_gain, ln_bias, gdn_w_in, gdn_conv_w, gdn_a_log, gdn_dt_bias, gdn_norm_w, gdn_w_out, sb_w_in, sb_w_out, lru_w_in, lru_conv_w, lru_conv_b, lru_w_gate_a, lru_b_gate_a, lru_w_gate_x, lru_b_gate_x, lru_lambda, lru_w_out, ffn_w_up, ffn_conv_w, ffn_conv_b, ffn_w_down):
    batch, seq_len, d = x.shape
    depth = ln_gain.shape[0]
    d_ff = ffn_w_down.shape[1]
    alpha = (2 * depth) ** 0.25
    xf = x.reshape(batch * seq_len, d)
    xb = xf.astype(BF16)
    for i in range(depth):
        kind, j = i % 3, i // 3
        if kind == 0:
            mix = _gdn_mixer(xb, gdn_w_in[j], gdn_conv_w[j], gdn_a_log[j], gdn_dt_bias[j], gdn_norm_w[j],
                             batch=batch, seq_len=seq_len)
            w_out = gdn_w_out[j]
        elif kind == 1:
            mix = _sb_mixer(xb, sb_w_in[j], batch=batch, seq_len=seq_len)
            w_out = sb_w_out[j]
        else:
            mix = _lru_mixer(xb, lru_w_in[j], lru_conv_w[j], lru_conv_b[j], lru_w_gate_a[j], lru_b_gate_a[j],
                             lru_w_gate_x[j], lru_b_gate_x[j], lru_lambda[j], batch=batch, seq_len=seq_len)
            w_out = lru_w_out[j]
        xf, xb = _outproj_ln(mix, w_out.astype(BF16), xf, ln_gain[i, 0], ln_bias[i, 0], alpha)
        hid = _proj(xb, ffn_w_up[i].astype(BF16), col0=0, ncols=d_ff, seq_len=seq_len, out_dtype=BF16,
                    cw=ffn_conv_w[i], bias=ffn_conv_b[i], act="silu", gate_col0=d_ff)
        xf, xb = _outproj_ln(hid, ffn_w_down[i].astype(BF16), xf, ln_gain[i, 1], ln_bias[i, 1], alpha)
    return xf.reshape(batch, seq_len, d)
```

```python
import functools
import math

import jax
import jax.numpy as jnp
from jax import lax
from jax.experimental import pallas as pl
from jax.experimental.pallas import tpu as pltpu

F32 = jnp.float32
BF16 = jnp.bfloat16

LN_EPS = 1e-5
RMS_EPS = 1e-6
HEAD_DIM = 128
GDN_CHUNK = 64
GDN_HEADS_PER_STEP = 4
SB_HEADS_PER_STEP = 4
LRU_BLOCK_DIM = 256
LRU_C = 8.0
SUBLANES = 8
LANES = 128
LOG2_E = 1.4426950408889634
MXU_PAIR_COLS = 512
VMEM_LIMIT_BYTES = 56 * 1024 * 1024
LHS_TILE_BYTES = 12 * 1024 * 1024


def _pick(n, candidates):
    for c in candidates:
        if n % c == 0:
            return c
    raise ValueError(f"no tile in {candidates} divides {n}")


def _params(n_axes):
    return pltpu.CompilerParams(dimension_semantics=("arbitrary",) * n_axes,
                                vmem_limit_bytes=VMEM_LIMIT_BYTES)


def _softplus(x):
    return jnp.maximum(x, 0.0) + jnp.log1p(jnp.exp(-jnp.abs(x)))


def _dot(a, b):
    return jnp.dot(a, b, preferred_element_type=F32)


def _dot_nt(a, b):
    return lax.dot_general(a, b, (((1,), (1,)), ((), ())), preferred_element_type=F32)


def _proj_kernel(*refs, taps, has_bias, act, gated, alpha, lead, tm, tn, tiles_per_seq):
    it = iter(refs)
    x_ref = next(it)
    w_ref = next(it)
    cw_ref = next(it) if taps else None
    b_ref = next(it) if has_bias else None
    r_ref = next(it) if alpha is not None else None
    o_ref = next(it)
    ext_ref = next(it) if taps else None
    halo_ref = next(it) if taps else None

    i = pl.program_id(0)
    j = pl.program_id(1)
    acc = _dot(x_ref[...], w_ref[...])
    if gated:
        up = acc[:, tn:]
        acc = acc[:, :tn]
    if taps:
        seq_start = (i % tiles_per_seq) == 0

        @pl.when(seq_start)
        def _():
            ext_ref[0:SUBLANES, :] = jnp.zeros((SUBLANES, tn), F32)

        @pl.when(jnp.logical_not(seq_start))
        def _():
            ext_ref[0:SUBLANES, :] = halo_ref[j]

        ext_ref[SUBLANES:, :] = acc
        halo_ref[j] = acc[tm - SUBLANES:, :]
        out = acc * cw_ref[taps - 1:taps, :]
        for s in range(1, taps):
            out = out + ext_ref[SUBLANES - s:SUBLANES - s + tm, :] * cw_ref[taps - 1 - s:taps - s, :]
    else:
        out = acc
    if has_bias:
        out = out + b_ref[...]
    if act == "silu":
        out = out * jax.nn.sigmoid(out)
    elif act == "gelu":
        out = jax.nn.gelu(out, approximate=True)
    if gated:
        out = out * up
    if alpha is not None:
        out = alpha * r_ref[...] + out
    if lead is not None:
        out = out * jnp.where(j < lead[0], lead[1], 1.0)
    o_ref[...] = out.astype(o_ref.dtype)


def _proj(xb, w, *, col0, ncols, seq_len, out_dtype, name, cw=None, bias=None, act="none", gated=False,
          resid=None, alpha=None, lead_scale=None):
    m, k = xb.shape
    tn = _pick(ncols, (MXU_PAIR_COLS // 2,) if gated else (MXU_PAIR_COLS, 256, 128))
    wn = 2 * tn if gated else tn
    tm = _pick(seq_len, tuple(t for t in (1024, 512, 256, 128) if t * k * 2 <= LHS_TILE_BYTES) or (128,))
    taps = 0 if cw is None else cw.shape[0]
    nj = ncols // tn
    c0 = col0 // tn
    assert col0 % tn == 0 and m % seq_len == 0
    in_specs = [pl.BlockSpec((tm, k), lambda i, j: (i, 0)),
                pl.BlockSpec((k, wn), lambda i, j: (0, c0 + j))]
    args = [xb, w]
    scratch = []
    if taps:
        in_specs.append(pl.BlockSpec((taps, tn), lambda i, j: (0, j)))
        args.append(cw)
        scratch = [pltpu.VMEM((tm + SUBLANES, tn), F32), pltpu.VMEM((nj, SUBLANES, tn), F32)]
    if bias is not None:
        in_specs.append(pl.BlockSpec((1, tn), lambda i, j: (0, j)))
        args.append(bias.reshape(1, ncols))
    if resid is not None:
        in_specs.append(pl.BlockSpec((tm, tn), lambda i, j: (i, j)))
        args.append(resid)
    lead = None
    if lead_scale is not None:
        assert lead_scale[0] % tn == 0
        lead = (lead_scale[0] // tn, lead_scale[1])
    kern = functools.partial(_proj_kernel, taps=taps, has_bias=bias is not None, act=act, gated=gated,
                             alpha=alpha, lead=lead, tm=tm, tn=tn, tiles_per_seq=seq_len // tm)
    return pl.pallas_call(
        kern,
        grid=(m // tm, nj),
        in_specs=in_specs,
        out_specs=pl.BlockSpec((tm, tn), lambda i, j: (i, j)),
        out_shape=jax.ShapeDtypeStruct((m, ncols), out_dtype),
        scratch_shapes=scratch,
        compiler_params=_params(2),
        name=name,
    )(*args)


def _pair_columns(w, tn):
    k, f2 = w.shape
    nj = f2 // 2 // tn
    return w.reshape(k, 2, nj, tn).transpose(0, 2, 1, 3).reshape(k, f2).astype(BF16)


def _ln_kernel(y_ref, g_ref, b_ref, of_ref, ob_ref):
    y = y_ref[...]
    mu = jnp.mean(y, axis=-1, keepdims=True)
    yc = y - mu
    var = jnp.mean(yc * yc, axis=-1, keepdims=True)
    out = yc * lax.rsqrt(var + LN_EPS) * g_ref[...] + b_ref[...]
    of_ref[...] = out
    ob_ref[...] = out.astype(BF16)


def _layer_norm(y, gain, bias):
    m, d = y.shape
    tm = _pick(m, (256, 128))
    return pl.pallas_call(
        _ln_kernel,
        grid=(m // tm,),
        in_specs=[pl.BlockSpec((tm, d), lambda i: (i, 0)),
                  pl.BlockSpec((1, d), lambda i: (0, 0)),
                  pl.BlockSpec((1, d), lambda i: (0, 0))],
        out_specs=[pl.BlockSpec((tm, d), lambda i: (i, 0)),
                   pl.BlockSpec((tm, d), lambda i: (i, 0))],
        out_shape=[jax.ShapeDtypeStruct((m, d), F32), jax.ShapeDtypeStruct((m, d), BF16)],
        compiler_params=_params(1),
        name="layer_norm",
    )(y, gain.reshape(1, d), bias.reshape(1, d))


def _gdn_gates_kernel(x_ref, w_ref, alog_ref, dtb_ref, gates_ref, gsuf_ref, *, heads, chunk):
    p = _dot(x_ref[...], w_ref[...])
    tm = p.shape[0]
    lane = lax.broadcasted_iota(jnp.int32, p.shape, 1)
    pos = lax.broadcasted_iota(jnp.int32, p.shape, 0) % chunk
    g = -jnp.exp(alog_ref[...]) * _softplus(p + dtb_ref[...])
    pre = g
    suf = g
    s = 1
    while s < chunk:
        pre = pre + jnp.where(pos >= s, pltpu.roll(pre, s, 0), 0.0)
        suf = suf + jnp.where(pos < chunk - s, pltpu.roll(suf, tm - s, 0), 0.0)
        s *= 2
    gates_ref[...] = jnp.where(lane < heads, jax.nn.sigmoid(p), pre)
    gsuf_ref[...] = suf - g


def _gdn_gates(xb, w_ba, a_log, dt_bias, heads):
    m, k = xb.shape
    tm = _pick(m, (512, 256, 128))
    pad = LANES - 2 * heads
    wp = jnp.pad(w_ba, ((0, 0), (0, pad)))
    alog = jnp.pad(a_log.reshape(1, heads), ((0, 0), (heads, pad)))
    dtb = jnp.pad(dt_bias.reshape(1, heads), ((0, 0), (heads, pad)))
    kern = functools.partial(_gdn_gates_kernel, heads=heads, chunk=GDN_CHUNK)
    return pl.pallas_call(
        kern,
        grid=(m // tm,),
        in_specs=[pl.BlockSpec((tm, k), lambda i: (i, 0)),
                  pl.BlockSpec((k, LANES), lambda i: (0, 0)),
                  pl.BlockSpec((1, LANES), lambda i: (0, 0)),
                  pl.BlockSpec((1, LANES), lambda i: (0, 0))],
        out_specs=[pl.BlockSpec((tm, LANES), lambda i: (i, 0)),
                   pl.BlockSpec((tm, LANES), lambda i: (i, 0))],
        out_shape=[jax.ShapeDtypeStruct((m, LANES), F32), jax.ShapeDtypeStruct((m, LANES), F32)],
        compiler_params=_params(1),
        name="gdn_gates",
    )(xb, wp, alog, dtb)


def _gdn_kernel(q_ref, k_ref, v_ref, z_ref, gates_ref, gsuf_ref, gcrow_ref, nw_ref, o_ref, s_ref, *,
                heads, chunk, nh):
    hg = pl.program_id(1)
    t = pl.program_id(2)

    @pl.when(t == 0)
    def _():
        s_ref[...] = jnp.zeros(s_ref.shape, F32)

    tc = q_ref.shape[0]
    dk = HEAD_DIM
    hs = range(nh)
    shift = int(math.log2(chunk))
    lane = lax.broadcasted_iota(jnp.int32, (tc, LANES), 1)
    rows = lax.broadcasted_iota(jnp.int32, (tc, tc), 0)
    cols = lax.broadcasted_iota(jnp.int32, (tc, tc), 1)
    same = (rows >> shift) == (cols >> shift)
    incl = same & (rows >= cols)
    strict = same & (rows > cols)
    eye = jnp.where(rows == cols, 1.0, 0.0)
    gts = gates_ref[...]
    gsf = gsuf_ref[...]

    def col(x, idx):
        return jnp.sum(jnp.where(lane == idx, x, 0.0), axis=1, keepdims=True)

    sl = [slice(h * dk, (h + 1) * dk) for h in hs]
    beta = [col(gts, hg * nh + h) for h in hs]
    gc = [col(gts, heads + hg * nh + h) for h in hs]
    gs = [col(gsf, heads + hg * nh + h) for h in hs]
    q = [q_ref[:, sl[h]] for h in hs]
    k = [k_ref[:, sl[h]] for h in hs]
    v = [v_ref[:, sl[h]] for h in hs]
    qn = [q[h] * lax.rsqrt(jnp.sum(q[h] * q[h], axis=-1, keepdims=True) + RMS_EPS) * (dk ** -0.5) for h in hs]
    kn = [k[h] * lax.rsqrt(jnp.sum(k[h] * k[h], axis=-1, keepdims=True) + RMS_EPS) for h in hs]
    kb = [kn[h] * beta[h] for h in hs]
    decay = [jnp.where(incl, jnp.exp(jnp.where(incl, gc[h] - gcrow_ref[h], 0.0)), 0.0) for h in hs]
    kn16 = [kn[h].astype(BF16) for h in hs]
    a = [jnp.where(strict, _dot_nt(kb[h].astype(BF16), kn16[h]) * decay[h], 0.0) for h in hs]
    qk = [(_dot_nt(qn[h].astype(BF16), kn16[h]) * decay[h]).astype(BF16) for h in hs]

    p = [eye - a[h] for h in hs]
    ak = [a[h].astype(BF16) for h in hs]
    for _ in range(shift - 1):
        ak = [_dot(ak[h], ak[h]).astype(BF16) for h in hs]
        p = [p[h] + _dot(p[h].astype(BF16), ak[h]) for h in hs]

    egc = [jnp.exp(gc[h]) for h in hs]
    rhs = [jnp.concatenate([v[h] * beta[h], kb[h] * egc[h]], axis=1).astype(BF16) for h in hs]
    sol = [_dot(p[h].astype(BF16), rhs[h]) for h in hs]
    u = [sol[h][:, :dk] for h in hs]
    w = [sol[h][:, dk:] for h in hs]
    q_dec = [qn[h] * egc[h] for h in hs]
    k_tail = [kn[h] * jnp.exp(gs[h]) for h in hs]

    state = [s_ref[h] for h in hs]
    outs = [[] for _ in hs]
    for c in range(tc // chunk):
        lo, hi = c * chunk, (c + 1) * chunk
        lhs = [jnp.concatenate([w[h][lo:hi], q_dec[h][lo:hi]], axis=0).astype(BF16) for h in hs]
        r = [_dot(lhs[h], state[h].astype(BF16)) for h in hs]
        v_new = [u[h][lo:hi] - r[h][:chunk] for h in hs]
        v_pad = []
        for h in hs:
            pieces = ([jnp.zeros((lo, dk), F32)] if lo else []) + [v_new[h]]
            pieces += [jnp.zeros((tc - hi, dk), F32)] if tc - hi else []
            v_pad.append(jnp.concatenate(pieces, axis=0).astype(BF16) if len(pieces) > 1
                         else v_new[h].astype(BF16))
        for h in hs:
            outs[h].append(r[h][chunk:] + _dot(qk[h][lo:hi], v_pad[h]))
        state = [state[h] * jnp.exp(gc[h][hi - 1:hi, :])
                 + _dot(k_tail[h][lo:hi].T.astype(BF16), v_new[h].astype(BF16)) for h in hs]
    for h in hs:
        s_ref[h] = state[h]
        o = jnp.concatenate(outs[h], axis=0)
        o = o * lax.rsqrt(jnp.mean(o * o, axis=-1, keepdims=True) + RMS_EPS) * nw_ref[...]
        zz = z_ref[:, sl[h]]
        o_ref[:, sl[h]] = (o * (zz * jax.nn.sigmoid(zz))).astype(o_ref.dtype)


def _gdn_core(qkv, z, gates, gsuf, norm_w, *, batch, seq_len, heads):
    m = qkv.shape[0]
    tc = _pick(seq_len, (256, 128, 64))
    nt = seq_len // tc
    nh = math.gcd(heads, GDN_HEADS_PER_STEP)
    ng = heads // nh
    gc_rows = gates[:, heads:2 * heads].T.reshape(ng, nh, 1, m)
    kern = functools.partial(_gdn_kernel, heads=heads, chunk=GDN_CHUNK, nh=nh)

    def tok(col):
        return pl.BlockSpec((tc, nh * HEAD_DIM), lambda b, h, t: (b * nt + t, col * ng + h))

    return pl.pallas_call(
        kern,
        grid=(batch, ng, nt),
        in_specs=[tok(0), tok(1), tok(2),
                  pl.BlockSpec((tc, nh * HEAD_DIM), lambda b, h, t: (b * nt + t, h)),
                  pl.BlockSpec((tc, LANES), lambda b, h, t: (b * nt + t, 0)),
                  pl.BlockSpec((tc, LANES), lambda b, h, t: (b * nt + t, 0)),
                  pl.BlockSpec((None, nh, 1, tc), lambda b, h, t: (h, 0, 0, b * nt + t)),
                  pl.BlockSpec((1, HEAD_DIM), lambda b, h, t: (0, 0))],
        out_specs=pl.BlockSpec((tc, nh * HEAD_DIM), lambda b, h, t: (b * nt + t, h)),
        out_shape=jax.ShapeDtypeStruct((m, heads * HEAD_DIM), BF16),
        scratch_shapes=[pltpu.VMEM((nh, HEAD_DIM, HEAD_DIM), F32)],
        compiler_params=_params(3),
        name="gdn_core",
    )(qkv, qkv, qkv, z, gates, gsuf, gc_rows, norm_w.reshape(1, HEAD_DIM))


def _gdn_mixer(xb, w_in, conv_w, a_log, dt_bias, norm_w, *, batch, seq_len):
    d = xb.shape[1]
    heads = d // HEAD_DIM
    w16 = w_in.astype(BF16)
    qkv = _proj(xb, w16, col0=0, ncols=3 * d, seq_len=seq_len, out_dtype=F32, cw=conv_w, act="silu",
                name="gdn_qkv_proj")
    z = _proj(xb, w16, col0=3 * d, ncols=d, seq_len=seq_len, out_dtype=F32, name="gdn_z_proj")
    gates, gsuf = _gdn_gates(xb, w16[:, 4 * d:], a_log, dt_bias, heads)
    return _gdn_core(qkv, z, gates, gsuf, norm_w, batch=batch, seq_len=seq_len, heads=heads)


def _sb_kernel(q_ref, k_ref, v_ref, o_ref, *, blk, nh):
    i = pl.program_id(2)
    hs = range(nh)
    sl = [slice(h * HEAD_DIM, (h + 1) * HEAD_DIM) for h in hs]
    q = [q_ref[:, sl[h]] for h in hs]
    rows = lax.broadcasted_iota(jnp.int32, (2 * blk, blk), 0)
    cols = lax.broadcasted_iota(jnp.int32, (2 * blk, blk), 1)
    after2 = jnp.where((rows & (blk - 1)) > cols, 1.0, 0.0).astype(BF16)
    before = (lax.broadcasted_iota(jnp.int32, (blk, blk), 1)
              < lax.broadcasted_iota(jnp.int32, (blk, blk), 0))

    def block(j, carry, masked):
        accs, runs = carry
        start = pl.multiple_of(j * blk, blk)
        nz = [_dot_nt(q[h], k_ref[pl.ds(start, blk), sl[h]]) for h in hs]
        log_keep = [jnp.minimum(nz[h], 0.0) - jnp.log(1.0 + jnp.exp2(jnp.abs(nz[h]) * -LOG2_E)) for h in hs]
        if masked:
            log_keep = [jnp.where(before, log_keep[h], 0.0) for h in hs]
        hi = [log_keep[h].astype(BF16) for h in hs]
        lo = [(log_keep[h] - hi[h].astype(F32)).astype(BF16) for h in hs]
        keep_after = [_dot(jnp.concatenate([hi[h], lo[h]], axis=1), after2) for h in hs]
        wgt = [jnp.exp(log_keep[h] - nz[h] + keep_after[h] + runs[h]) for h in hs]
        if masked:
            wgt = [jnp.where(before, wgt[h], 0.0) for h in hs]
        accs = [accs[h] + _dot(wgt[h].astype(BF16), v_ref[pl.ds(start, blk), sl[h]]) for h in hs]
        runs = [runs[h] + keep_after[h][:, 0:1] + log_keep[h][:, 0:1] for h in hs]
        return accs, runs

    carry = ([jnp.zeros((blk, HEAD_DIM), F32) for _ in hs], [jnp.zeros((blk, 1), F32) for _ in hs])
    carry = block(i, carry, True)
    accs, _ = lax.fori_loop(0, i, lambda jj, c: block(i - 1 - jj, c, False), carry)
    for h in hs:
        o_ref[:, sl[h]] = accs[h].astype(o_ref.dtype)


def _sb_attention(qkv, *, batch, seq_len, heads):
    m = qkv.shape[0]
    blk = _pick(seq_len, (256, 128))
    nq = seq_len // blk
    nh = math.gcd(heads, SB_HEADS_PER_STEP)
    ng = heads // nh
    kern = functools.partial(_sb_kernel, blk=blk, nh=nh)
    return pl.pallas_call(
        kern,
        grid=(batch, ng, nq),
        in_specs=[pl.BlockSpec((blk, nh * HEAD_DIM), lambda b, h, i: (b * nq + i, h)),
                  pl.BlockSpec((seq_len, nh * HEAD_DIM), lambda b, h, i: (b, ng + h)),
                  pl.BlockSpec((seq_len, nh * HEAD_DIM), lambda b, h, i: (b, 2 * ng + h))],
        out_specs=pl.BlockSpec((blk, nh * HEAD_DIM), lambda b, h, i: (b * nq + i, h)),
        out_shape=jax.ShapeDtypeStruct((m, heads * HEAD_DIM), BF16),
        compiler_params=_params(3),
        name="sb_attention",
    )(qkv, qkv, qkv)


def _sb_mixer(xb, w_in, *, batch, seq_len):
    d = xb.shape[1]
    qkv = _proj(xb, w_in.astype(BF16), col0=0, ncols=3 * d, seq_len=seq_len, out_dtype=BF16, name="sb_qkv_proj",
                lead_scale=(d, -(HEAD_DIM ** -0.5)))
    return _sb_attention(qkv, batch=batch, seq_len=seq_len, heads=d // HEAD_DIM)


def _lru_kernel(xr_ref, y_ref, wg_ref, bg_ref, lam_ref, o_ref, h_ref, a_scr, u_scr):
    t = pl.program_id(2)

    @pl.when(t == 0)
    def _():
        h_ref[...] = jnp.zeros(h_ref.shape, F32)

    x = xr_ref[...]
    tt, c = x.shape
    gates = jax.nn.sigmoid(_dot(x.astype(BF16), wg_ref[...]) + bg_ref[...])
    log_a = -LRU_C * gates[:, :c] * _softplus(-lam_ref[...])
    a = jnp.exp(log_a)
    a_scr[...] = a
    u_scr[...] = jnp.sqrt(-jnp.tanh(log_a) * (a * a + 1.0)) * gates[:, c:] * x

    pos = lax.broadcasted_iota(jnp.int32, (SUBLANES, c), 0)

    def slab(s, h):
        r0 = pl.multiple_of(s * SUBLANES, SUBLANES)
        a8 = a_scr[pl.ds(r0, SUBLANES), :]
        u8 = u_scr[pl.ds(r0, SUBLANES), :]
        d = 1
        while d < SUBLANES:
            a_prev = jnp.where(pos >= d, pltpu.roll(a8, d, 0), 1.0)
            u_prev = jnp.where(pos >= d, pltpu.roll(u8, d, 0), 0.0)
            u8 = a8 * u_prev + u8
            a8 = a8 * a_prev
            d *= 2
        h8 = a8 * h + u8
        u_scr[pl.ds(r0, SUBLANES), :] = h8
        return h8[SUBLANES - 1:SUBLANES, :]

    h_ref[...] = lax.fori_loop(0, tt // SUBLANES, slab, h_ref[...])
    o_ref[...] = (u_scr[...] * y_ref[...]).astype(o_ref.dtype)


def _lru_core(xr, y, w_gate_a, b_gate_a, w_gate_x, b_gate_x, lam, *, batch, seq_len):
    m, d = xr.shape
    c = LRU_BLOCK_DIM
    nb = d // c
    tt = _pick(seq_len, (512, 256, 128))
    nt = seq_len // tt
    w_gates = jnp.concatenate([w_gate_a, w_gate_x], axis=-1).astype(BF16)
    b_gates = jnp.concatenate([b_gate_a, b_gate_x], axis=-1).reshape(nb, 1, 2 * c)

    def tok():
        return pl.BlockSpec((tt, c), lambda b, n, t: (b * nt + t, n))

    def per_block(rows, width):
        return pl.BlockSpec((None, rows, width), lambda b, n, t: (n, 0, 0))

    return pl.pallas_call(
        _lru_kernel,
        grid=(batch, nb, nt),
        in_specs=[tok(), tok(), per_block(c, 2 * c), per_block(1, 2 * c), per_block(1, c)],
        out_specs=tok(),
        out_shape=jax.ShapeDtypeStruct((m, d), BF16),
        scratch_shapes=[pltpu.VMEM((1, c), F32), pltpu.VMEM((tt, c), F32), pltpu.VMEM((tt, c), F32)],
        compiler_params=_params(3),
        name="lru_core",
    )(xr, y, w_gates, b_gates, lam.reshape(nb, 1, c))


def _lru_mixer(xb, w_in, conv_w, conv_b, w_gate_a, b_gate_a, w_gate_x, b_gate_x, lam, *, batch, seq_len):
    d = xb.shape[1]
    w16 = w_in.astype(BF16)
    xr = _proj(xb, w16, col0=0, ncols=d, seq_len=seq_len, out_dtype=F32, cw=conv_w, bias=conv_b,
               name="lru_x_proj")
    y = _proj(xb, w16, col0=d, ncols=d, seq_len=seq_len, out_dtype=F32, act="gelu", name="lru_y_proj")
    return _lru_core(xr, y, w_gate_a, b_gate_a, w_gate_x, b_gate_x, lam, batch=batch, seq_len=seq_len)


def kernel(x, ln_gain, ln_bias, gdn_w_in, gdn_conv_w, gdn_a_log, gdn_dt_bias, gdn_norm_w, gdn_w_out, sb_w_in, sb_w_out, lru_w_in, lru_conv_w, lru_conv_b, lru_w_gate_a, lru_b_gate_a, lru_w_gate_x, lru_b_gate_x, lru_lambda, lru_w_out, ffn_w_up, ffn_conv_w, ffn_conv_b, ffn_w_down):
    batch, seq_len, d = x.shape
    depth = ln_gain.shape[0]
    d_ff = ffn_w_down.shape[1]
    alpha = (2 * depth) ** 0.25
    xf = x.reshape(batch * seq_len, d)
    xb = xf.astype(BF16)
    for i in range(depth):
        kind, j = i % 3, i // 3
        if kind == 0:
            mix = _gdn_mixer(xb, gdn_w_in[j], gdn_conv_w[j], gdn_a_log[j], gdn_dt_bias[j], gdn_norm_w[j],
                             batch=batch, seq_len=seq_len)
            w_out = gdn_w_out[j]
        elif kind == 1:
            mix = _sb_mixer(xb, sb_w_in[j], batch=batch, seq_len=seq_len)
            w_out = sb_w_out[j]
        else:
            mix = _lru_mixer(xb, lru_w_in[j], lru_conv_w[j], lru_conv_b[j], lru_w_gate_a[j], lru_b_gate_a[j],
                             lru_w_gate_x[j], lru_b_gate_x[j], lru_lambda[j], batch=batch, seq_len=seq_len)
            w_out = lru_w_out[j]
        y = _proj(mix, w_out.astype(BF16), col0=0, ncols=d, seq_len=seq_len, out_dtype=F32, resid=xf, alpha=alpha,
                  name="mixer_out_proj")
        xf, xb = _layer_norm(y, ln_gain[i, 0], ln_bias[i, 0])
        hid = _proj(xb, _pair_columns(ffn_w_up[i], MXU_PAIR_COLS // 2), col0=0, ncols=d_ff, seq_len=seq_len,
                    out_dtype=BF16, cw=ffn_conv_w[i], bias=ffn_conv_b[i], act="silu", gated=True, name="ffn_up")
        y = _proj(hid, ffn_w_down[i].astype(BF16), col0=0, ncols=d, seq_len=seq_len, out_dtype=F32, resid=xf,
                  alpha=alpha, name="ffn_down")
        xf, xb = _layer_norm(y, ln_gain[i, 1], ln_bias[i, 1])
    return xf.reshape(batch, seq_len, d)
```

```python
import functools
import math

import jax
import jax.numpy as jnp
from jax import lax
from jax.experimental import pallas as pl
from jax.experimental.pallas import tpu as pltpu

F32 = jnp.float32
BF16 = jnp.bfloat16

LN_EPS = 1e-5
RMS_EPS = 1e-6
HEAD_DIM = 128
GDN_CHUNK = 64
GDN_HEADS_PER_STEP = 8
SB_HEADS_PER_STEP = 4
LRU_BLOCK_DIM = 256
LRU_C = 8.0
SUBLANES = 8
LANES = 128
LOG2_E = 1.4426950408889634
MXU_PAIR_COLS = 512
VMEM_LIMIT_BYTES = 56 * 1024 * 1024
LHS_TILE_BYTES = 12 * 1024 * 1024


def _pick(n, candidates):
    for c in candidates:
        if n % c == 0:
            return c
    raise ValueError(f"no tile in {candidates} divides {n}")


def _params(n_axes):
    return pltpu.CompilerParams(dimension_semantics=("arbitrary",) * n_axes,
                                vmem_limit_bytes=VMEM_LIMIT_BYTES)


def _softplus(x):
    return jnp.maximum(x, 0.0) + jnp.log1p(jnp.exp(-jnp.abs(x)))


def _dot(a, b):
    return jnp.dot(a, b, preferred_element_type=F32)


def _dot_nt(a, b):
    return lax.dot_general(a, b, (((1,), (1,)), ((), ())), preferred_element_type=F32)


def _proj_kernel(*refs, taps, has_bias, act, gated, alpha, lead, tm, tn, tiles_per_seq):
    it = iter(refs)
    x_ref = next(it)
    w_ref = next(it)
    cw_ref = next(it) if taps else None
    b_ref = next(it) if has_bias else None
    r_ref = next(it) if alpha is not None else None
    o_ref = next(it)
    ext_ref = next(it) if taps else None
    halo_ref = next(it) if taps else None

    i = pl.program_id(0)
    j = pl.program_id(1)
    if taps:
        seq_start = (i % tiles_per_seq) == 0

        @pl.when(seq_start)
        def _():
            ext_ref[0:SUBLANES, :] = jnp.zeros((SUBLANES, tn), F32)

        @pl.when(jnp.logical_not(seq_start))
        def _():
            ext_ref[0:SUBLANES, :] = halo_ref[j]

    acc = _dot(x_ref[...], w_ref[...])
    if gated:
        up = acc[:, tn:]
        acc = acc[:, :tn]
    if taps:
        ext_ref[SUBLANES:, :] = acc
        halo_ref[j] = acc[tm - SUBLANES:, :]
        out = acc * cw_ref[taps - 1:taps, :]
        for s in range(1, taps):
            out = out + ext_ref[SUBLANES - s:SUBLANES - s + tm, :] * cw_ref[taps - 1 - s:taps - s, :]
    else:
        out = acc
    if has_bias:
        out = out + b_ref[...]
    if act == "silu":
        out = out * jax.nn.sigmoid(out)
    elif act == "gelu":
        out = jax.nn.gelu(out, approximate=True)
    if gated:
        out = out * up
    if alpha is not None:
        out = alpha * r_ref[...] + out
    if lead is not None:
        out = out * jnp.where(j < lead[0], lead[1], 1.0)
    o_ref[...] = out.astype(o_ref.dtype)


def _proj(xb, w, *, col0, ncols, seq_len, out_dtype, name, cw=None, bias=None, act="none", gated=False,
          resid=None, alpha=None, lead_scale=None):
    m, k = xb.shape
    tn = _pick(ncols, (MXU_PAIR_COLS // 2,) if gated else (MXU_PAIR_COLS, 256, 128))
    wn = 2 * tn if gated else tn
    tm = _pick(seq_len, tuple(t for t in (1024, 512, 256, 128) if t * k * 2 <= LHS_TILE_BYTES) or (128,))
    taps = 0 if cw is None else cw.shape[0]
    nj = ncols // tn
    c0 = col0 // tn
    assert col0 % tn == 0 and m % seq_len == 0
    in_specs = [pl.BlockSpec((tm, k), lambda i, j: (i, 0)),
                pl.BlockSpec((k, wn), lambda i, j: (0, c0 + j))]
    args = [xb, w]
    scratch = []
    if taps:
        in_specs.append(pl.BlockSpec((taps, tn), lambda i, j: (0, j)))
        args.append(cw)
        scratch = [pltpu.VMEM((tm + SUBLANES, tn), F32), pltpu.VMEM((nj, SUBLANES, tn), F32)]
    if bias is not None:
        in_specs.append(pl.BlockSpec((1, tn), lambda i, j: (0, j)))
        args.append(bias.reshape(1, ncols))
    if resid is not None:
        in_specs.append(pl.BlockSpec((tm, tn), lambda i, j: (i, j)))
        args.append(resid)
    lead = None
    if lead_scale is not None:
        assert lead_scale[0] % tn == 0
        lead = (lead_scale[0] // tn, lead_scale[1])
    kern = functools.partial(_proj_kernel, taps=taps, has_bias=bias is not None, act=act, gated=gated,
                             alpha=alpha, lead=lead, tm=tm, tn=tn, tiles_per_seq=seq_len // tm)
    return pl.pallas_call(
        kern,
        grid=(m // tm, nj),
        in_specs=in_specs,
        out_specs=pl.BlockSpec((tm, tn), lambda i, j: (i, j)),
        out_shape=jax.ShapeDtypeStruct((m, ncols), out_dtype),
        scratch_shapes=scratch,
        compiler_params=_params(2),
        name=name,
    )(*args)


def _pair_columns(w, tn):
    f = w.shape[1] // 2
    cols = []
    for c in range(0, f, tn):
        cols += [w[:, c:c + tn], w[:, f + c:f + c + tn]]
    return jnp.concatenate(cols, axis=1).astype(BF16)


def _ln_kernel(y_ref, g_ref, b_ref, of_ref, ob_ref):
    y = y_ref[...]
    mu = jnp.mean(y, axis=-1, keepdims=True)
    yc = y - mu
    var = jnp.mean(yc * yc, axis=-1, keepdims=True)
    out = yc * lax.rsqrt(var + LN_EPS) * g_ref[...] + b_ref[...]
    of_ref[...] = out
    ob_ref[...] = out.astype(BF16)


def _layer_norm(y, gain, bias):
    m, d = y.shape
    tm = _pick(m, (256, 128))
    return pl.pallas_call(
        _ln_kernel,
        grid=(m // tm,),
        in_specs=[pl.BlockSpec((tm, d), lambda i: (i, 0)),
                  pl.BlockSpec((1, d), lambda i: (0, 0)),
                  pl.BlockSpec((1, d), lambda i: (0, 0))],
        out_specs=[pl.BlockSpec((tm, d), lambda i: (i, 0)),
                   pl.BlockSpec((tm, d), lambda i: (i, 0))],
        out_shape=[jax.ShapeDtypeStruct((m, d), F32), jax.ShapeDtypeStruct((m, d), BF16)],
        compiler_params=_params(1),
        name="layer_norm",
    )(y, gain.reshape(1, d), bias.reshape(1, d))


def _gdn_gates_kernel(x_ref, w_ref, alog_ref, dtb_ref, gates_ref, gsuf_ref, *, heads, chunk):
    p = _dot(x_ref[...], w_ref[...])
    tm = p.shape[0]
    lane = lax.broadcasted_iota(jnp.int32, p.shape, 1)
    pos = lax.broadcasted_iota(jnp.int32, p.shape, 0) % chunk
    g = -jnp.exp(alog_ref[...]) * _softplus(p + dtb_ref[...])
    pre = g
    suf = g
    s = 1
    while s < chunk:
        pre = pre + jnp.where(pos >= s, pltpu.roll(pre, s, 0), 0.0)
        suf = suf + jnp.where(pos < chunk - s, pltpu.roll(suf, tm - s, 0), 0.0)
        s *= 2
    gates_ref[...] = jnp.where(lane < heads, jax.nn.sigmoid(p), pre)
    gsuf_ref[...] = suf - g


def _gdn_gates(xb, w_ba, a_log, dt_bias, heads):
    m, k = xb.shape
    tm = _pick(m, (512, 256, 128))
    pad = LANES - 2 * heads
    wp = jnp.pad(w_ba, ((0, 0), (0, pad)))
    alog = jnp.pad(a_log.reshape(1, heads), ((0, 0), (heads, pad)))
    dtb = jnp.pad(dt_bias.reshape(1, heads), ((0, 0), (heads, pad)))
    kern = functools.partial(_gdn_gates_kernel, heads=heads, chunk=GDN_CHUNK)
    return pl.pallas_call(
        kern,
        grid=(m // tm,),
        in_specs=[pl.BlockSpec((tm, k), lambda i: (i, 0)),
                  pl.BlockSpec((k, LANES), lambda i: (0, 0)),
                  pl.BlockSpec((1, LANES), lambda i: (0, 0)),
                  pl.BlockSpec((1, LANES), lambda i: (0, 0))],
        out_specs=[pl.BlockSpec((tm, LANES), lambda i: (i, 0)),
                   pl.BlockSpec((tm, LANES), lambda i: (i, 0))],
        out_shape=[jax.ShapeDtypeStruct((m, LANES), F32), jax.ShapeDtypeStruct((m, LANES), F32)],
        compiler_params=_params(1),
        name="gdn_gates",
    )(xb, wp, alog, dtb)


def _gdn_kernel(q_ref, k_ref, v_ref, z_ref, gates_ref, gsuf_ref, gcrow_ref, nw_ref, o_ref, s_ref, *,
                heads, chunk, nh):
    hg = pl.program_id(1)
    t = pl.program_id(2)

    @pl.when(t == 0)
    def _():
        s_ref[...] = jnp.zeros(s_ref.shape, F32)

    tc = q_ref.shape[0]
    dk = HEAD_DIM
    hs = range(nh)
    shift = int(math.log2(chunk))
    lane = lax.broadcasted_iota(jnp.int32, (tc, LANES), 1)
    rows = lax.broadcasted_iota(jnp.int32, (tc, tc), 0)
    cols = lax.broadcasted_iota(jnp.int32, (tc, tc), 1)
    same = (rows >> shift) == (cols >> shift)
    incl = same & (rows >= cols)
    strict = same & (rows > cols)
    eye = jnp.where(rows == cols, 1.0, 0.0)
    gts = gates_ref[...]
    gsf = gsuf_ref[...]

    def col(x, idx):
        return jnp.sum(jnp.where(lane == idx, x, 0.0), axis=1, keepdims=True)

    sl = [slice(h * dk, (h + 1) * dk) for h in hs]
    beta = [col(gts, hg * nh + h) for h in hs]
    gc = [col(gts, heads + hg * nh + h) for h in hs]
    gs = [col(gsf, heads + hg * nh + h) for h in hs]
    q = [q_ref[:, sl[h]] for h in hs]
    k = [k_ref[:, sl[h]] for h in hs]
    v = [v_ref[:, sl[h]] for h in hs]
    qn = [q[h] * lax.rsqrt(jnp.sum(q[h] * q[h], axis=-1, keepdims=True) + RMS_EPS) * (dk ** -0.5) for h in hs]
    kn = [k[h] * lax.rsqrt(jnp.sum(k[h] * k[h], axis=-1, keepdims=True) + RMS_EPS) for h in hs]
    kb = [kn[h] * beta[h] for h in hs]
    decay = [jnp.where(incl, jnp.exp(jnp.where(incl, gc[h] - gcrow_ref[h], 0.0)), 0.0) for h in hs]
    kn16 = [kn[h].astype(BF16) for h in hs]
    a = [jnp.where(strict, _dot_nt(kb[h].astype(BF16), kn16[h]) * decay[h], 0.0) for h in hs]
    qk = [(_dot_nt(qn[h].astype(BF16), kn16[h]) * decay[h]).astype(BF16) for h in hs]

    p = [eye - a[h] for h in hs]
    ak = [a[h].astype(BF16) for h in hs]
    for _ in range(shift - 1):
        ak = [_dot(ak[h], ak[h]).astype(BF16) for h in hs]
        p = [p[h] + _dot(p[h].astype(BF16), ak[h]) for h in hs]

    egc = [jnp.exp(gc[h]) for h in hs]
    rhs = [jnp.concatenate([v[h] * beta[h], kb[h] * egc[h]], axis=1).astype(BF16) for h in hs]
    sol = [_dot(p[h].astype(BF16), rhs[h]) for h in hs]
    u = [sol[h][:, :dk] for h in hs]
    w = [sol[h][:, dk:] for h in hs]
    q_dec = [qn[h] * egc[h] for h in hs]
    k_tail = [kn[h] * jnp.exp(gs[h]) for h in hs]

    state = [s_ref[h] for h in hs]
    outs = [[] for _ in hs]
    for c in range(tc // chunk):
        lo, hi = c * chunk, (c + 1) * chunk
        lhs = [jnp.concatenate([w[h][lo:hi], q_dec[h][lo:hi]], axis=0).astype(BF16) for h in hs]
        r = [_dot(lhs[h], state[h].astype(BF16)) for h in hs]
        v_new = [u[h][lo:hi] - r[h][:chunk] for h in hs]
        v_pad = []
        for h in hs:
            pieces = ([jnp.zeros((lo, dk), F32)] if lo else []) + [v_new[h]]
            pieces += [jnp.zeros((tc - hi, dk), F32)] if tc - hi else []
            v_pad.append(jnp.concatenate(pieces, axis=0).astype(BF16) if len(pieces) > 1
                         else v_new[h].astype(BF16))
        for h in hs:
            outs[h].append(r[h][chunk:] + _dot(qk[h][lo:hi], v_pad[h]))
        state = [state[h] * jnp.exp(gc[h][hi - 1:hi, :])
                 + _dot(k_tail[h][lo:hi].T.astype(BF16), v_new[h].astype(BF16)) for h in hs]
    for h in hs:
        s_ref[h] = state[h]
        o = jnp.concatenate(outs[h], axis=0)
        o = o * lax.rsqrt(jnp.mean(o * o, axis=-1, keepdims=True) + RMS_EPS) * nw_ref[...]
        zz = z_ref[:, sl[h]]
        o_ref[:, sl[h]] = (o * (zz * jax.nn.sigmoid(zz))).astype(o_ref.dtype)


def _gdn_core(qkv, z, gates, gsuf, norm_w, *, batch, seq_len, heads):
    m = qkv.shape[0]
    tc = _pick(seq_len, (256, 128, 64))
    nt = seq_len // tc
    nh = math.gcd(heads, GDN_HEADS_PER_STEP)
    ng = heads // nh
    gc_rows = gates[:, heads:2 * heads].T.reshape(ng, nh, 1, m)
    kern = functools.partial(_gdn_kernel, heads=heads, chunk=GDN_CHUNK, nh=nh)

    def tok(col):
        return pl.BlockSpec((tc, nh * HEAD_DIM), lambda b, h, t: (b * nt + t, col * ng + h))

    return pl.pallas_call(
        kern,
        grid=(batch, ng, nt),
        in_specs=[tok(0), tok(1), tok(2),
                  pl.BlockSpec((tc, nh * HEAD_DIM), lambda b, h, t: (b * nt + t, h)),
                  pl.BlockSpec((tc, LANES), lambda b, h, t: (b * nt + t, 0)),
                  pl.BlockSpec((tc, LANES), lambda b, h, t: (b * nt + t, 0)),
                  pl.BlockSpec((None, nh, 1, tc), lambda b, h, t: (h, 0, 0, b * nt + t)),
                  pl.BlockSpec((1, HEAD_DIM), lambda b, h, t: (0, 0))],
        out_specs=pl.BlockSpec((tc, nh * HEAD_DIM), lambda b, h, t: (b * nt + t, h)),
        out_shape=jax.ShapeDtypeStruct((m, heads * HEAD_DIM), BF16),
        scratch_shapes=[pltpu.VMEM((nh, HEAD_DIM, HEAD_DIM), F32)],
        compiler_params=_params(3),
        name="gdn_core",
    )(qkv, qkv, qkv, z, gates, gsuf, gc_rows, norm_w.reshape(1, HEAD_DIM))


def _gdn_mixer(xb, w_in, conv_w, a_log, dt_bias, norm_w, *, batch, seq_len):
    d = xb.shape[1]
    heads = d // HEAD_DIM
    w16 = w_in.astype(BF16)
    qkv = _proj(xb, w16, col0=0, ncols=3 * d, seq_len=seq_len, out_dtype=F32, cw=conv_w, act="silu",
                name="gdn_qkv_proj")
    z = _proj(xb, w16, col0=3 * d, ncols=d, seq_len=seq_len, out_dtype=F32, name="gdn_z_proj")
    gates, gsuf = _gdn_gates(xb, w16[:, 4 * d:], a_log, dt_bias, heads)
    return _gdn_core(qkv, z, gates, gsuf, norm_w, batch=batch, seq_len=seq_len, heads=heads)


def _sb_kernel(q_ref, k_ref, v_ref, o_ref, *, blk, nh):
    i = pl.program_id(2)
    hs = range(nh)
    sl = [slice(h * HEAD_DIM, (h + 1) * HEAD_DIM) for h in hs]
    q = [q_ref[:, sl[h]] for h in hs]
    rows = lax.broadcasted_iota(jnp.int32, (blk, blk), 0)
    cols = lax.broadcasted_iota(jnp.int32, (blk, blk), 1)
    after = jnp.where(rows > cols, 1.0, 0.0).astype(BF16)
    before = cols < rows

    def scores(j):
        start = pl.multiple_of(j * blk, blk)
        return [_dot_nt(q[h], k_ref[pl.ds(start, blk), sl[h]]) for h in hs]

    def block(j, nz, accs, runs, masked):
        start = pl.multiple_of(j * blk, blk)
        log_keep = [jnp.minimum(nz[h], 0.0) - jnp.log(1.0 + jnp.exp2(jnp.abs(nz[h]) * -LOG2_E)) for h in hs]
        if masked:
            log_keep = [jnp.where(before, log_keep[h], 0.0) for h in hs]
        keep_after = [_dot(log_keep[h].astype(BF16), after) for h in hs]
        wgt = [jnp.exp(log_keep[h] - nz[h] + keep_after[h] + runs[h]) for h in hs]
        if masked:
            wgt = [jnp.where(before, wgt[h], 0.0) for h in hs]
        accs = [accs[h] + _dot(wgt[h].astype(BF16), v_ref[pl.ds(start, blk), sl[h]]) for h in hs]
        runs = [runs[h] + keep_after[h][:, 0:1] + log_keep[h][:, 0:1] for h in hs]
        return accs, runs

    accs, runs = block(i, scores(i), [jnp.zeros((blk, HEAD_DIM), F32) for _ in hs],
                       [jnp.zeros((blk, 1), F32) for _ in hs], True)

    def body(jj, carry):
        j = i - 1 - jj
        return block(j, scores(j), carry[0], carry[1], False)

    accs, _ = lax.fori_loop(0, i, body, (accs, runs))
    for h in hs:
        o_ref[:, sl[h]] = accs[h].astype(o_ref.dtype)


def _sb_attention(qkv, *, batch, seq_len, heads):
    m = qkv.shape[0]
    blk = _pick(seq_len, (256, 128))
    nq = seq_len // blk
    nh = math.gcd(heads, SB_HEADS_PER_STEP)
    ng = heads // nh
    kern = functools.partial(_sb_kernel, blk=blk, nh=nh)
    return pl.pallas_call(
        kern,
        grid=(batch, ng, nq),
        in_specs=[pl.BlockSpec((blk, nh * HEAD_DIM), lambda b, h, i: (b * nq + i, h)),
                  pl.BlockSpec((seq_len, nh * HEAD_DIM), lambda b, h, i: (b, ng + h)),
                  pl.BlockSpec((seq_len, nh * HEAD_DIM), lambda b, h, i: (b, 2 * ng + h))],
        out_specs=pl.BlockSpec((blk, nh * HEAD_DIM), lambda b, h, i: (b * nq + i, h)),
        out_shape=jax.ShapeDtypeStruct((m, heads * HEAD_DIM), BF16),
        compiler_params=_params(3),
        name="sb_attention",
    )(qkv, qkv, qkv)


def _sb_mixer(xb, w_in, *, batch, seq_len):
    d = xb.shape[1]
    qkv = _proj(xb, w_in.astype(BF16), col0=0, ncols=3 * d, seq_len=seq_len, out_dtype=BF16, name="sb_qkv_proj",
                lead_scale=(d, -(HEAD_DIM ** -0.5)))
    return _sb_attention(qkv, batch=batch, seq_len=seq_len, heads=d // HEAD_DIM)


def _lru_kernel(xr_ref, y_ref, wg_ref, bg_ref, lam_ref, o_ref, h_ref, a_scr, u_scr):
    t = pl.program_id(2)

    @pl.when(t == 0)
    def _():
        h_ref[...] = jnp.zeros(h_ref.shape, F32)

    x = xr_ref[...]
    tt, c = x.shape
    gates = jax.nn.sigmoid(_dot(x.astype(BF16), wg_ref[...]) + bg_ref[...])
    log_a = -LRU_C * gates[:, :c] * _softplus(-lam_ref[...])
    a = jnp.exp(log_a)
    a_scr[...] = a
    u_scr[...] = jnp.sqrt(-jnp.tanh(log_a) * (a * a + 1.0)) * gates[:, c:] * x

    pos = lax.broadcasted_iota(jnp.int32, (SUBLANES, c), 0)

    def slab(s, h):
        r0 = pl.multiple_of(s * SUBLANES, SUBLANES)
        a8 = a_scr[pl.ds(r0, SUBLANES), :]
        u8 = u_scr[pl.ds(r0, SUBLANES), :]
        d = 1
        while d < SUBLANES:
            a_prev = jnp.where(pos >= d, pltpu.roll(a8, d, 0), 1.0)
            u_prev = jnp.where(pos >= d, pltpu.roll(u8, d, 0), 0.0)
            u8 = a8 * u_prev + u8
            a8 = a8 * a_prev
            d *= 2
        h8 = a8 * h + u8
        u_scr[pl.ds(r0, SUBLANES), :] = h8
        return h8[SUBLANES - 1:SUBLANES, :]

    h_ref[...] = lax.fori_loop(0, tt // SUBLANES, slab, h_ref[...])
    o_ref[...] = (u_scr[...] * y_ref[...]).astype(o_ref.dtype)


def _lru_core(xr, y, w_gate_a, b_gate_a, w_gate_x, b_gate_x, lam, *, batch, seq_len):
    m, d = xr.shape
    c = LRU_BLOCK_DIM
    nb = d // c
    tt = _pick(seq_len, (512, 256, 128))
    nt = seq_len // tt
    w_gates = jnp.concatenate([w_gate_a, w_gate_x], axis=-1).astype(BF16)
    b_gates = jnp.concatenate([b_gate_a, b_gate_x], axis=-1).reshape(nb, 1, 2 * c)

    def tok():
        return pl.BlockSpec((tt, c), lambda b, n, t: (b * nt + t, n))

    def per_block(rows, width):
        return pl.BlockSpec((None, rows, width), lambda b, n, t: (n, 0, 0))

    return pl.pallas_call(
        _lru_kernel,
        grid=(batch, nb, nt),
        in_specs=[tok(), tok(), per_block(c, 2 * c), per_block(1, 2 * c), per_block(1, c)],
        out_specs=tok(),
        out_shape=jax.ShapeDtypeStruct((m, d), BF16),
        scratch_shapes=[pltpu.VMEM((1, c), F32), pltpu.VMEM((tt, c), F32), pltpu.VMEM((tt, c), F32)],
        compiler_params=_params(3),
        name="lru_core",
    )(xr, y, w_gates, b_gates, lam.reshape(nb, 1, c))


def _lru_mixer(xb, w_in, conv_w, conv_b, w_gate_a, b_gate_a, w_gate_x, b_gate_x, lam, *, batch, seq_len):
    d = xb.shape[1]
    w16 = w_in.astype(BF16)
    xr = _proj(xb, w16, col0=0, ncols=d, seq_len=seq_len, out_dtype=F32, cw=conv_w, bias=conv_b,
               name="lru_x_proj")
    y = _proj(xb, w16, col0=d, ncols=d, seq_len=seq_len, out_dtype=F32, act="gelu", name="lru_y_proj")
    return _lru_core(xr, y, w_gate_a, b_gate_a, w_gate_x, b_gate_x, lam, batch=batch, seq_len=seq_len)


def kernel(x, ln_gain, ln_bias, gdn_w_in, gdn_conv_w, gdn_a_log, gdn_dt_bias, gdn_norm_w, gdn_w_out, sb_w_in, sb_w_out, lru_w_in, lru_conv_w, lru_conv_b, lru_w_gate_a, lru_b_gate_a, lru_w_gate_x, lru_b_gate_x, lru_lambda, lru_w_out, ffn_w_up, ffn_conv_w, ffn_conv_b, ffn_w_down):
    batch, seq_len, d = x.shape
    depth = ln_gain.shape[0]
    d_ff = ffn_w_down.shape[1]
    alpha = (2 * depth) ** 0.25
    xf = x.reshape(batch * seq_len, d)
    xb = xf.astype(BF16)
    for i in range(depth):
        kind, j = i % 3, i // 3
        if kind == 0:
            mix = _gdn_mixer(xb, gdn_w_in[j], gdn_conv_w[j], gdn_a_log[j], gdn_dt_bias[j], gdn_norm_w[j],
                             batch=batch, seq_len=seq_len)
            w_out = gdn_w_out[j]
        elif kind == 1:
            mix = _sb_mixer(xb, sb_w_in[j], batch=batch, seq_len=seq_len)
            w_out = sb_w_out[j]
        else:
            mix = _lru_mixer(xb, lru_w_in[j], lru_conv_w[j], lru_conv_b[j], lru_w_gate_a[j], lru_b_gate_a[j],
                             lru_w_gate_x[j], lru_b_gate_x[j], lru_lambda[j], batch=batch, seq_len=seq_len)
            w_out = lru_w_out[j]
        y = _proj(mix, w_out.astype(BF16), col0=0, ncols=d, seq_len=seq_len, out_dtype=F32, resid=xf, alpha=alpha,
                  name="mixer_out_proj")
        xf, xb = _layer_norm(y, ln_gain[i, 0], ln_bias[i, 0])
        hid = _proj(xb, _pair_columns(ffn_w_up[i], MXU_PAIR_COLS // 2), col0=0, ncols=d_ff, seq_len=seq_len,
                    out_dtype=BF16, cw=ffn_conv_w[i], bias=ffn_conv_b[i], act="silu", gated=True, name="ffn_up")
        y = _proj(hid, ffn_w_down[i].astype(BF16), col0=0, ncols=d, seq_len=seq_len, out_dtype=F32, resid=xf,
                  alpha=alpha, name="ffn_down")
        xf, xb = _layer_norm(y, ln_gain[i, 1], ln_bias[i, 1])
    return xf.reshape(batch, seq_len, d)
```

```python
import functools
import math

import jax
import jax.numpy as jnp
from jax import lax
from jax.experimental import pallas as pl
from jax.experimental.pallas import tpu as pltpu

F32 = jnp.float32
BF16 = jnp.bfloat16

LN_EPS = 1e-5
RMS_EPS = 1e-6
HEAD_DIM = 128
GDN_CHUNK = 64
GDN_HEADS_PER_STEP = 8
SB_HEADS_PER_STEP = 4
LRU_BLOCK_DIM = 256
LRU_C = 8.0
SUBLANES = 8
LANES = 128
LOG2_E = 1.4426950408889634
MXU_PAIR_COLS = 512
VMEM_LIMIT_BYTES = 56 * 1024 * 1024
LHS_TILE_BYTES = 12 * 1024 * 1024


def _pick(n, candidates):
    for c in candidates:
        if n % c == 0:
            return c
    raise ValueError(f"no tile in {candidates} divides {n}")


def _params(n_axes):
    return pltpu.CompilerParams(dimension_semantics=("arbitrary",) * n_axes,
                                vmem_limit_bytes=VMEM_LIMIT_BYTES)


def _softplus(x):
    return jnp.maximum(x, 0.0) + jnp.log1p(jnp.exp(-jnp.abs(x)))


def _dot(a, b):
    return jnp.dot(a, b, preferred_element_type=F32)


def _dot_nt(a, b):
    return lax.dot_general(a, b, (((1,), (1,)), ((), ())), preferred_element_type=F32)


def _proj_kernel(*refs, taps, has_bias, act, gated, alpha, lead, tm, tn, tiles_per_seq):
    it = iter(refs)
    x_ref = next(it)
    w_ref = next(it)
    cw_ref = next(it) if taps else None
    b_ref = next(it) if has_bias else None
    r_ref = next(it) if alpha is not None else None
    o_ref = next(it)
    ext_ref = next(it) if taps else None
    halo_ref = next(it) if taps else None

    i = pl.program_id(0)
    j = pl.program_id(1)
    if taps:
        seq_start = (i % tiles_per_seq) == 0

        @pl.when(seq_start)
        def _():
            ext_ref[0:SUBLANES, :] = jnp.zeros((SUBLANES, tn), F32)

        @pl.when(jnp.logical_not(seq_start))
        def _():
            ext_ref[0:SUBLANES, :] = halo_ref[j]

    acc = _dot(x_ref[...], w_ref[...])
    if gated:
        up = acc[:, tn:]
        acc = acc[:, :tn]
    if taps:
        ext_ref[SUBLANES:, :] = acc
        halo_ref[j] = acc[tm - SUBLANES:, :]
        out = acc * cw_ref[taps - 1:taps, :]
        for s in range(1, taps):
            out = out + ext_ref[SUBLANES - s:SUBLANES - s + tm, :] * cw_ref[taps - 1 - s:taps - s, :]
    else:
        out = acc
    if has_bias:
        out = out + b_ref[...]
    if act == "silu":
        out = out * jax.nn.sigmoid(out)
    elif act == "gelu":
        out = jax.nn.gelu(out, approximate=True)
    if gated:
        out = out * up
    if alpha is not None:
        out = alpha * r_ref[...] + out
    if lead is not None:
        out = out * jnp.where(j < lead[0], lead[1], 1.0)
    o_ref[...] = out.astype(o_ref.dtype)


def _proj(xb, w, layer, *, col0, ncols, seq_len, out_dtype, name, cw=None, bias=None, act="none", gated=False,
          resid=None, alpha=None, lead_scale=None):
    m, k = xb.shape
    tn = _pick(ncols, (MXU_PAIR_COLS // 2,) if gated else (MXU_PAIR_COLS, 256, 128))
    wn = 2 * tn if gated else tn
    tm = _pick(seq_len, tuple(t for t in (1024, 512, 256, 128) if t * k * 2 <= LHS_TILE_BYTES) or (128,))
    taps = 0 if cw is None else cw.shape[0]
    nj = ncols // tn
    c0 = col0 // tn
    assert col0 % tn == 0 and m % seq_len == 0
    in_specs = [pl.BlockSpec((tm, k), lambda i, j: (i, 0)),
                pl.BlockSpec((None, k, wn), lambda i, j: (layer, 0, c0 + j))]
    args = [xb, w]
    scratch = []
    if taps:
        in_specs.append(pl.BlockSpec((taps, tn), lambda i, j: (0, j)))
        args.append(cw)
        scratch = [pltpu.VMEM((tm + SUBLANES, tn), F32), pltpu.VMEM((nj, SUBLANES, tn), F32)]
    if bias is not None:
        in_specs.append(pl.BlockSpec((1, tn), lambda i, j: (0, j)))
        args.append(bias.reshape(1, ncols))
    if resid is not None:
        in_specs.append(pl.BlockSpec((tm, tn), lambda i, j: (i, j)))
        args.append(resid)
    lead = None
    if lead_scale is not None:
        assert lead_scale[0] % tn == 0
        lead = (lead_scale[0] // tn, lead_scale[1])
    kern = functools.partial(_proj_kernel, taps=taps, has_bias=bias is not None, act=act, gated=gated,
                             alpha=alpha, lead=lead, tm=tm, tn=tn, tiles_per_seq=seq_len // tm)
    return pl.pallas_call(
        kern,
        grid=(m // tm, nj),
        in_specs=in_specs,
        out_specs=pl.BlockSpec((tm, tn), lambda i, j: (i, j)),
        out_shape=jax.ShapeDtypeStruct((m, ncols), out_dtype),
        scratch_shapes=scratch,
        compiler_params=_params(2),
        name=name,
    )(*args)


def _pair_kernel(g_ref, u_ref, o_ref):
    tn = g_ref.shape[1]
    o_ref[:, :tn] = g_ref[...].astype(BF16)
    o_ref[:, tn:] = u_ref[...].astype(BF16)


def _pair_columns(w, tn):
    layers, k, f2 = w.shape
    nj = f2 // 2 // tn
    return pl.pallas_call(
        _pair_kernel,
        grid=(layers, nj),
        in_specs=[pl.BlockSpec((None, k, tn), lambda l, j: (l, 0, j)),
                  pl.BlockSpec((None, k, tn), lambda l, j: (l, 0, nj + j))],
        out_specs=pl.BlockSpec((None, k, 2 * tn), lambda l, j: (l, 0, j)),
        out_shape=jax.ShapeDtypeStruct((layers, k, f2), BF16),
        compiler_params=_params(2),
        name="pair_gate_up",
    )(w, w)


def _ln_kernel(y_ref, g_ref, b_ref, of_ref, ob_ref):
    y = y_ref[...]
    mu = jnp.mean(y, axis=-1, keepdims=True)
    yc = y - mu
    var = jnp.mean(yc * yc, axis=-1, keepdims=True)
    out = yc * lax.rsqrt(var + LN_EPS) * g_ref[...] + b_ref[...]
    of_ref[...] = out
    ob_ref[...] = out.astype(BF16)


def _layer_norm(y, gain, bias):
    m, d = y.shape
    tm = _pick(m, (256, 128))
    return pl.pallas_call(
        _ln_kernel,
        grid=(m // tm,),
        in_specs=[pl.BlockSpec((tm, d), lambda i: (i, 0)),
                  pl.BlockSpec((1, d), lambda i: (0, 0)),
                  pl.BlockSpec((1, d), lambda i: (0, 0))],
        out_specs=[pl.BlockSpec((tm, d), lambda i: (i, 0)),
                   pl.BlockSpec((tm, d), lambda i: (i, 0))],
        out_shape=[jax.ShapeDtypeStruct((m, d), F32), jax.ShapeDtypeStruct((m, d), BF16)],
        compiler_params=_params(1),
        name="layer_norm",
    )(y, gain.reshape(1, d), bias.reshape(1, d))


def _gdn_gates_kernel(x_ref, w_ref, alog_ref, dtb_ref, gates_ref, gsuf_ref, *, heads, chunk):
    p = _dot(x_ref[...], w_ref[...])
    tm = p.shape[0]
    lane = lax.broadcasted_iota(jnp.int32, p.shape, 1)
    pos = lax.broadcasted_iota(jnp.int32, p.shape, 0) % chunk
    g = -jnp.exp(alog_ref[...]) * _softplus(p + dtb_ref[...])
    pre = g
    suf = g
    s = 1
    while s < chunk:
        pre = pre + jnp.where(pos >= s, pltpu.roll(pre, s, 0), 0.0)
        suf = suf + jnp.where(pos < chunk - s, pltpu.roll(suf, tm - s, 0), 0.0)
        s *= 2
    gates_ref[...] = jnp.where(lane < heads, jax.nn.sigmoid(p), pre)
    gsuf_ref[...] = suf - g


def _gdn_gates(xb, w_ba, a_log, dt_bias, heads):
    m, k = xb.shape
    tm = _pick(m, (512, 256, 128))
    pad = LANES - 2 * heads
    wp = jnp.pad(w_ba, ((0, 0), (0, pad)))
    alog = jnp.pad(a_log.reshape(1, heads), ((0, 0), (heads, pad)))
    dtb = jnp.pad(dt_bias.reshape(1, heads), ((0, 0), (heads, pad)))
    kern = functools.partial(_gdn_gates_kernel, heads=heads, chunk=GDN_CHUNK)
    return pl.pallas_call(
        kern,
        grid=(m // tm,),
        in_specs=[pl.BlockSpec((tm, k), lambda i: (i, 0)),
                  pl.BlockSpec((k, LANES), lambda i: (0, 0)),
                  pl.BlockSpec((1, LANES), lambda i: (0, 0)),
                  pl.BlockSpec((1, LANES), lambda i: (0, 0))],
        out_specs=[pl.BlockSpec((tm, LANES), lambda i: (i, 0)),
                   pl.BlockSpec((tm, LANES), lambda i: (i, 0))],
        out_shape=[jax.ShapeDtypeStruct((m, LANES), F32), jax.ShapeDtypeStruct((m, LANES), F32)],
        compiler_params=_params(1),
        name="gdn_gates",
    )(xb, wp, alog, dtb)


def _gdn_kernel(q_ref, k_ref, v_ref, z_ref, gates_ref, gsuf_ref, gcrow_ref, nw_ref, o_ref, s_ref, *,
                heads, chunk, nh):
    hg = pl.program_id(1)
    t = pl.program_id(2)

    @pl.when(t == 0)
    def _():
        s_ref[...] = jnp.zeros(s_ref.shape, F32)

    tc = q_ref.shape[0]
    dk = HEAD_DIM
    hs = range(nh)
    shift = int(math.log2(chunk))
    lane = lax.broadcasted_iota(jnp.int32, (tc, LANES), 1)
    rows = lax.broadcasted_iota(jnp.int32, (tc, tc), 0)
    cols = lax.broadcasted_iota(jnp.int32, (tc, tc), 1)
    same = (rows >> shift) == (cols >> shift)
    incl = same & (rows >= cols)
    strict = same & (rows > cols)
    eye = jnp.where(rows == cols, 1.0, 0.0)
    gts = gates_ref[...]
    gsf = gsuf_ref[...]

    def col(x, idx):
        return jnp.sum(jnp.where(lane == idx, x, 0.0), axis=1, keepdims=True)

    sl = [slice(h * dk, (h + 1) * dk) for h in hs]
    beta = [col(gts, hg * nh + h) for h in hs]
    gc = [col(gts, heads + hg * nh + h) for h in hs]
    gs = [col(gsf, heads + hg * nh + h) for h in hs]
    q = [q_ref[:, sl[h]] for h in hs]
    k = [k_ref[:, sl[h]] for h in hs]
    v = [v_ref[:, sl[h]] for h in hs]
    qn = [q[h] * lax.rsqrt(jnp.sum(q[h] * q[h], axis=-1, keepdims=True) + RMS_EPS) * (dk ** -0.5) for h in hs]
    kn = [k[h] * lax.rsqrt(jnp.sum(k[h] * k[h], axis=-1, keepdims=True) + RMS_EPS) for h in hs]
    kb = [kn[h] * beta[h] for h in hs]
    decay = [jnp.where(incl, jnp.exp(jnp.where(incl, gc[h] - gcrow_ref[h], 0.0)), 0.0) for h in hs]
    kn16 = [kn[h].astype(BF16) for h in hs]
    a = [jnp.where(strict, _dot_nt(kb[h].astype(BF16), kn16[h]) * decay[h], 0.0) for h in hs]
    qk = [(_dot_nt(qn[h].astype(BF16), kn16[h]) * decay[h]).astype(BF16) for h in hs]

    p = [eye - a[h] for h in hs]
    ak = [a[h].astype(BF16) for h in hs]
    for _ in range(shift - 1):
        ak = [_dot(ak[h], ak[h]).astype(BF16) for h in hs]
        p = [p[h] + _dot(p[h].astype(BF16), ak[h]) for h in hs]

    egc = [jnp.exp(gc[h]) for h in hs]
    rhs = [jnp.concatenate([v[h] * beta[h], kb[h] * egc[h]], axis=1).astype(BF16) for h in hs]
    sol = [_dot(p[h].astype(BF16), rhs[h]) for h in hs]
    u = [sol[h][:, :dk] for h in hs]
    w = [sol[h][:, dk:] for h in hs]
    q_dec = [qn[h] * egc[h] for h in hs]
    k_tail = [kn[h] * jnp.exp(gs[h]) for h in hs]

    state = [s_ref[h] for h in hs]
    outs = [[] for _ in hs]
    for c in range(tc // chunk):
        lo, hi = c * chunk, (c + 1) * chunk
        lhs = [jnp.concatenate([w[h][lo:hi], q_dec[h][lo:hi]], axis=0).astype(BF16) for h in hs]
        r = [_dot(lhs[h], state[h].astype(BF16)) for h in hs]
        v_new = [u[h][lo:hi] - r[h][:chunk] for h in hs]
        v_pad = []
        for h in hs:
            pieces = ([jnp.zeros((lo, dk), F32)] if lo else []) + [v_new[h]]
            pieces += [jnp.zeros((tc - hi, dk), F32)] if tc - hi else []
            v_pad.append(jnp.concatenate(pieces, axis=0).astype(BF16) if len(pieces) > 1
                         else v_new[h].astype(BF16))
        for h in hs:
            outs[h].append(r[h][chunk:] + _dot(qk[h][lo:hi], v_pad[h]))
        state = [state[h] * jnp.exp(gc[h][hi - 1:hi, :])
                 + _dot(k_tail[h][lo:hi].T.astype(BF16), v_new[h].astype(BF16)) for h in hs]
    for h in hs:
        s_ref[h] = state[h]
        o = jnp.concatenate(outs[h], axis=0)
        o = o * lax.rsqrt(jnp.mean(o * o, axis=-1, keepdims=True) + RMS_EPS) * nw_ref[...]
        zz = z_ref[:, sl[h]]
        o_ref[:, sl[h]] = (o * (zz * jax.nn.sigmoid(zz))).astype(o_ref.dtype)


def _gdn_core(qkv, z, gates, gsuf, norm_w, *, batch, seq_len, heads):
    m = qkv.shape[0]
    tc = _pick(seq_len, (256, 128, 64))
    nt = seq_len // tc
    nh = math.gcd(heads, GDN_HEADS_PER_STEP)
    ng = heads // nh
    gc_rows = gates[:, heads:2 * heads].T.reshape(ng, nh, 1, m)
    kern = functools.partial(_gdn_kernel, heads=heads, chunk=GDN_CHUNK, nh=nh)

    def tok(col):
        return pl.BlockSpec((tc, nh * HEAD_DIM), lambda b, h, t: (b * nt + t, col * ng + h))

    return pl.pallas_call(
        kern,
        grid=(batch, ng, nt),
        in_specs=[tok(0), tok(1), tok(2),
                  pl.BlockSpec((tc, nh * HEAD_DIM), lambda b, h, t: (b * nt + t, h)),
                  pl.BlockSpec((tc, LANES), lambda b, h, t: (b * nt + t, 0)),
                  pl.BlockSpec((tc, LANES), lambda b, h, t: (b * nt + t, 0)),
                  pl.BlockSpec((None, nh, 1, tc), lambda b, h, t: (h, 0, 0, b * nt + t)),
                  pl.BlockSpec((1, HEAD_DIM), lambda b, h, t: (0, 0))],
        out_specs=pl.BlockSpec((tc, nh * HEAD_DIM), lambda b, h, t: (b * nt + t, h)),
        out_shape=jax.ShapeDtypeStruct((m, heads * HEAD_DIM), BF16),
        scratch_shapes=[pltpu.VMEM((nh, HEAD_DIM, HEAD_DIM), F32)],
        compiler_params=_params(3),
        name="gdn_core",
    )(qkv, qkv, qkv, z, gates, gsuf, gc_rows, norm_w.reshape(1, HEAD_DIM))


def _gdn_mixer(xb, w16, layer, conv_w, a_log, dt_bias, norm_w, *, batch, seq_len):
    d = xb.shape[1]
    heads = d // HEAD_DIM
    qkv = _proj(xb, w16, layer, col0=0, ncols=3 * d, seq_len=seq_len, out_dtype=F32, cw=conv_w, act="silu",
                name="gdn_qkv_proj")
    z = _proj(xb, w16, layer, col0=3 * d, ncols=d, seq_len=seq_len, out_dtype=F32, name="gdn_z_proj")
    gates, gsuf = _gdn_gates(xb, w16[layer, :, 4 * d:], a_log, dt_bias, heads)
    return _gdn_core(qkv, z, gates, gsuf, norm_w, batch=batch, seq_len=seq_len, heads=heads)


def _sb_kernel(q_ref, k_ref, v_ref, o_ref, *, blk, nh):
    i = pl.program_id(2)
    hs = range(nh)
    sl = [slice(h * HEAD_DIM, (h + 1) * HEAD_DIM) for h in hs]
    q = [q_ref[:, sl[h]] for h in hs]
    rows = lax.broadcasted_iota(jnp.int32, (blk, blk), 0)
    cols = lax.broadcasted_iota(jnp.int32, (blk, blk), 1)
    after = jnp.where(rows > cols, 1.0, 0.0).astype(BF16)
    before = cols < rows

    def scores(j):
        start = pl.multiple_of(j * blk, blk)
        return [_dot_nt(q[h], k_ref[pl.ds(start, blk), sl[h]]) for h in hs]

    def block(j, nz, accs, runs, masked):
        start = pl.multiple_of(j * blk, blk)
        log_keep = [jnp.minimum(nz[h], 0.0) - jnp.log(1.0 + jnp.exp2(jnp.abs(nz[h]) * -LOG2_E)) for h in hs]
        if masked:
            log_keep = [jnp.where(before, log_keep[h], 0.0) for h in hs]
        keep_after = [_dot(log_keep[h].astype(BF16), after) for h in hs]
        wgt = [jnp.exp(log_keep[h] - nz[h] + keep_after[h] + runs[h]) for h in hs]
        if masked:
            wgt = [jnp.where(before, wgt[h], 0.0) for h in hs]
        accs = [accs[h] + _dot(wgt[h].astype(BF16), v_ref[pl.ds(start, blk), sl[h]]) for h in hs]
        runs = [runs[h] + keep_after[h][:, 0:1] + log_keep[h][:, 0:1] for h in hs]
        return accs, runs

    accs, runs = block(i, scores(i), [jnp.zeros((blk, HEAD_DIM), F32) for _ in hs],
                       [jnp.zeros((blk, 1), F32) for _ in hs], True)

    def body(jj, carry):
        j = i - 1 - jj
        return block(j, scores(j), carry[0], carry[1], False)

    accs, _ = lax.fori_loop(0, i, body, (accs, runs))
    for h in hs:
        o_ref[:, sl[h]] = accs[h].astype(o_ref.dtype)


def _sb_attention(qkv, *, batch, seq_len, heads):
    m = qkv.shape[0]
    blk = _pick(seq_len, (256, 128))
    nq = seq_len // blk
    nh = math.gcd(heads, SB_HEADS_PER_STEP)
    ng = heads // nh
    kern = functools.partial(_sb_kernel, blk=blk, nh=nh)
    return pl.pallas_call(
        kern,
        grid=(batch, ng, nq),
        in_specs=[pl.BlockSpec((blk, nh * HEAD_DIM), lambda b, h, i: (b * nq + i, h)),
                  pl.BlockSpec((seq_len, nh * HEAD_DIM), lambda b, h, i: (b, ng + h)),
                  pl.BlockSpec((seq_len, nh * HEAD_DIM), lambda b, h, i: (b, 2 * ng + h))],
        out_specs=pl.BlockSpec((blk, nh * HEAD_DIM), lambda b, h, i: (b * nq + i, h)),
        out_shape=jax.ShapeDtypeStruct((m, heads * HEAD_DIM), BF16),
        compiler_params=_params(3),
        name="sb_attention",
    )(qkv, qkv, qkv)


def _sb_mixer(xb, w16, layer, *, batch, seq_len):
    d = xb.shape[1]
    qkv = _proj(xb, w16, layer, col0=0, ncols=3 * d, seq_len=seq_len, out_dtype=BF16, name="sb_qkv_proj",
                lead_scale=(d, -(HEAD_DIM ** -0.5)))
    return _sb_attention(qkv, batch=batch, seq_len=seq_len, heads=d // HEAD_DIM)


def _lru_kernel(xr_ref, y_ref, wg_ref, bg_ref, lam_ref, o_ref, h_ref, a_scr, u_scr):
    t = pl.program_id(2)

    @pl.when(t == 0)
    def _():
        h_ref[...] = jnp.zeros(h_ref.shape, F32)

    x = xr_ref[...]
    tt, c = x.shape
    gates = jax.nn.sigmoid(_dot(x.astype(BF16), wg_ref[...]) + bg_ref[...])
    log_a = -LRU_C * gates[:, :c] * _softplus(-lam_ref[...])
    a = jnp.exp(log_a)
    a_scr[...] = a
    u_scr[...] = jnp.sqrt(-jnp.tanh(log_a) * (a * a + 1.0)) * gates[:, c:] * x

    pos = lax.broadcasted_iota(jnp.int32, (SUBLANES, c), 0)

    def slab(s, h):
        r0 = pl.multiple_of(s * SUBLANES, SUBLANES)
        a8 = a_scr[pl.ds(r0, SUBLANES), :]
        u8 = u_scr[pl.ds(r0, SUBLANES), :]
        d = 1
        while d < SUBLANES:
            a_prev = jnp.where(pos >= d, pltpu.roll(a8, d, 0), 1.0)
            u_prev = jnp.where(pos >= d, pltpu.roll(u8, d, 0), 0.0)
            u8 = a8 * u_prev + u8
            a8 = a8 * a_prev
            d *= 2
        h8 = a8 * h + u8
        u_scr[pl.ds(r0, SUBLANES), :] = h8
        return h8[SUBLANES - 1:SUBLANES, :]

    h_ref[...] = lax.fori_loop(0, tt // SUBLANES, slab, h_ref[...])
    o_ref[...] = (u_scr[...] * y_ref[...]).astype(o_ref.dtype)


def _lru_core(xr, y, w_gate_a, b_gate_a, w_gate_x, b_gate_x, lam, *, batch, seq_len):
    m, d = xr.shape
    c = LRU_BLOCK_DIM
    nb = d // c
    tt = _pick(seq_len, (512, 256, 128))
    nt = seq_len // tt
    w_gates = jnp.concatenate([w_gate_a, w_gate_x], axis=-1).astype(BF16)
    b_gates = jnp.concatenate([b_gate_a, b_gate_x], axis=-1).reshape(nb, 1, 2 * c)

    def tok():
        return pl.BlockSpec((tt, c), lambda b, n, t: (b * nt + t, n))

    def per_block(rows, width):
        return pl.BlockSpec((None, rows, width), lambda b, n, t: (n, 0, 0))

    return pl.pallas_call(
        _lru_kernel,
        grid=(batch, nb, nt),
        in_specs=[tok(), tok(), per_block(c, 2 * c), per_block(1, 2 * c), per_block(1, c)],
        out_specs=tok(),
        out_shape=jax.ShapeDtypeStruct((m, d), BF16),
        scratch_shapes=[pltpu.VMEM((1, c), F32), pltpu.VMEM((tt, c), F32), pltpu.VMEM((tt, c), F32)],
        compiler_params=_params(3),
        name="lru_core",
    )(xr, y, w_gates, b_gates, lam.reshape(nb, 1, c))


def _lru_mixer(xb, w16, layer, conv_w, conv_b, w_gate_a, b_gate_a, w_gate_x, b_gate_x, lam, *, batch, seq_len):
    d = xb.shape[1]
    xr = _proj(xb, w16, layer, col0=0, ncols=d, seq_len=seq_len, out_dtype=F32, cw=conv_w, bias=conv_b,
               name="lru_x_proj")
    y = _proj(xb, w16, layer, col0=d, ncols=d, seq_len=seq_len, out_dtype=F32, act="gelu", name="lru_y_proj")
    return _lru_core(xr, y, w_gate_a, b_gate_a, w_gate_x, b_gate_x, lam, batch=batch, seq_len=seq_len)


def kernel(x, ln_gain, ln_bias, gdn_w_in, gdn_conv_w, gdn_a_log, gdn_dt_bias, gdn_norm_w, gdn_w_out, sb_w_in, sb_w_out, lru_w_in, lru_conv_w, lru_conv_b, lru_w_gate_a, lru_b_gate_a, lru_w_gate_x, lru_b_gate_x, lru_lambda, lru_w_out, ffn_w_up, ffn_conv_w, ffn_conv_b, ffn_w_down):
    batch, seq_len, d = x.shape
    depth = ln_gain.shape[0]
    d_ff = ffn_w_down.shape[1]
    alpha = (2 * depth) ** 0.25
    xf = x.reshape(batch * seq_len, d)
    xb = xf.astype(BF16)
    gdn_in, gdn_out = gdn_w_in.astype(BF16), gdn_w_out.astype(BF16)
    sb_in, sb_out = sb_w_in.astype(BF16), sb_w_out.astype(BF16)
    lru_in, lru_out = lru_w_in.astype(BF16), lru_w_out.astype(BF16)
    ffn_up = _pair_columns(ffn_w_up, MXU_PAIR_COLS // 2)
    ffn_down = ffn_w_down.astype(BF16)
    for i in range(depth):
        kind, j = i % 3, i // 3
        if kind == 0:
            mix = _gdn_mixer(xb, gdn_in, j, gdn_conv_w[j], gdn_a_log[j], gdn_dt_bias[j], gdn_norm_w[j],
                             batch=batch, seq_len=seq_len)
            w_out = gdn_out
        elif kind == 1:
            mix = _sb_mixer(xb, sb_in, j, batch=batch, seq_len=seq_len)
            w_out = sb_out
        else:
            mix = _lru_mixer(xb, lru_in, j, lru_conv_w[j], lru_conv_b[j], lru_w_gate_a[j], lru_b_gate_a[j],
                             lru_w_gate_x[j], lru_b_gate_x[j], lru_lambda[j], batch=batch, seq_len=seq_len)
            w_out = lru_out
        y = _proj(mix, w_out, j, col0=0, ncols=d, seq_len=seq_len, out_dtype=F32, resid=xf, alpha=alpha,
                  name="mixer_out_proj")
        xf, xb = _layer_norm(y, ln_gain[i, 0], ln_bias[i, 0])
        hid = _proj(xb, ffn_up, i, col0=0, ncols=d_ff, seq_len=seq_len, out_dtype=BF16, cw=ffn_conv_w[i],
                    bias=ffn_conv_b[i], act="silu", gated=True, name="ffn_up")
        y = _proj(hid, ffn_down, i, col0=0, ncols=d, seq_len=seq_len, out_dtype=F32, resid=xf, alpha=alpha,
                  name="ffn_down")
        xf, xb = _layer_norm(y, ln_gain[i, 1], ln_bias[i, 1])
    return xf.reshape(batch, seq_len, d)
```

```python
import functools
import math

import jax
import jax.numpy as jnp
from jax import lax
from jax.experimental import pallas as pl
from jax.experimental.pallas import tpu as pltpu

F32 = jnp.float32
BF16 = jnp.bfloat16

LN_EPS = 1e-5
RMS_EPS = 1e-6
HEAD_DIM = 128
GDN_CHUNK = 64
GDN_HEADS_PER_STEP = 8
SB_HEADS_PER_STEP = 4
LRU_BLOCK_DIM = 256
LRU_C = 8.0
SUBLANES = 8
LANES = 128
LOG2_E = 1.4426950408889634
MXU_PAIR_COLS = 512
VMEM_LIMIT_BYTES = 56 * 1024 * 1024
LHS_TILE_BYTES = 12 * 1024 * 1024


def _pick(n, candidates):
    for c in candidates:
        if n % c == 0:
            return c
    raise ValueError(f"no tile in {candidates} divides {n}")


def _params(n_axes):
    return pltpu.CompilerParams(dimension_semantics=("arbitrary",) * n_axes,
                                vmem_limit_bytes=VMEM_LIMIT_BYTES)


def _softplus(x):
    return jnp.maximum(x, 0.0) + jnp.log1p(jnp.exp(-jnp.abs(x)))


def _dot(a, b):
    return jnp.dot(a, b, preferred_element_type=F32)


def _dot_nt(a, b):
    return lax.dot_general(a, b, (((1,), (1,)), ((), ())), preferred_element_type=F32)


def _proj_kernel(*refs, taps, has_bias, act, gated, alpha, lead, tm, tn, tiles_per_seq):
    it = iter(refs)
    x_ref = next(it)
    w_ref = next(it)
    cw_ref = next(it) if taps else None
    b_ref = next(it) if has_bias else None
    r_ref = next(it) if alpha is not None else None
    o_ref = next(it)
    ext_ref = next(it) if taps else None
    halo_ref = next(it) if taps else None

    i = pl.program_id(0)
    j = pl.program_id(1)
    if taps:
        seq_start = (i % tiles_per_seq) == 0

        @pl.when(seq_start)
        def _():
            ext_ref[0:SUBLANES, :] = jnp.zeros((SUBLANES, tn), F32)

        @pl.when(jnp.logical_not(seq_start))
        def _():
            ext_ref[0:SUBLANES, :] = halo_ref[j]

    acc = _dot(x_ref[...], w_ref[...])
    if gated:
        up = acc[:, tn:]
        acc = acc[:, :tn]
    if taps:
        ext_ref[SUBLANES:, :] = acc
        halo_ref[j] = acc[tm - SUBLANES:, :]
        out = acc * cw_ref[taps - 1:taps, :]
        for s in range(1, taps):
            out = out + ext_ref[SUBLANES - s:SUBLANES - s + tm, :] * cw_ref[taps - 1 - s:taps - s, :]
    else:
        out = acc
    if has_bias:
        out = out + b_ref[...]
    if act == "silu":
        out = out * jax.nn.sigmoid(out)
    elif act == "gelu":
        out = jax.nn.gelu(out, approximate=True)
    if gated:
        out = out * up
    if alpha is not None:
        out = alpha * r_ref[...] + out
    if lead is not None:
        out = out * jnp.where(j < lead[0], lead[1], 1.0)
    o_ref[...] = out.astype(o_ref.dtype)


def _proj(xb, w, layer, *, col0, ncols, seq_len, out_dtype, name, cw=None, bias=None, act="none", gated=False,
          resid=None, alpha=None, lead_scale=None):
    m, k = xb.shape
    tn = _pick(ncols, (MXU_PAIR_COLS // 2,) if gated else (MXU_PAIR_COLS, 256, 128))
    wn = 2 * tn if gated else tn
    tm = _pick(seq_len, tuple(t for t in (1024, 512, 256, 128) if t * k * 2 <= LHS_TILE_BYTES) or (128,))
    taps = 0 if cw is None else cw.shape[0]
    nj = ncols // tn
    c0 = col0 // tn
    assert col0 % tn == 0 and m % seq_len == 0
    in_specs = [pl.BlockSpec((tm, k), lambda i, j: (i, 0)),
                pl.BlockSpec((None, k, wn), lambda i, j: (layer, 0, c0 + j))]
    args = [xb, w]
    scratch = []
    if taps:
        in_specs.append(pl.BlockSpec((taps, tn), lambda i, j: (0, j)))
        args.append(cw)
        scratch = [pltpu.VMEM((tm + SUBLANES, tn), F32), pltpu.VMEM((nj, SUBLANES, tn), F32)]
    if bias is not None:
        in_specs.append(pl.BlockSpec((1, tn), lambda i, j: (0, j)))
        args.append(bias.reshape(1, ncols))
    if resid is not None:
        in_specs.append(pl.BlockSpec((tm, tn), lambda i, j: (i, j)))
        args.append(resid)
    lead = None
    if lead_scale is not None:
        assert lead_scale[0] % tn == 0
        lead = (lead_scale[0] // tn, lead_scale[1])
    kern = functools.partial(_proj_kernel, taps=taps, has_bias=bias is not None, act=act, gated=gated,
                             alpha=alpha, lead=lead, tm=tm, tn=tn, tiles_per_seq=seq_len // tm)
    return pl.pallas_call(
        kern,
        grid=(m // tm, nj),
        in_specs=in_specs,
        out_specs=pl.BlockSpec((tm, tn), lambda i, j: (i, j)),
        out_shape=jax.ShapeDtypeStruct((m, ncols), out_dtype),
        scratch_shapes=scratch,
        compiler_params=_params(2),
        name=name,
    )(*args)


def _pair_kernel(g_ref, u_ref, o_ref):
    tn = g_ref.shape[1]
    o_ref[:, :tn] = g_ref[...].astype(BF16)
    o_ref[:, tn:] = u_ref[...].astype(BF16)


def _pair_columns(w, tn):
    layers, k, f2 = w.shape
    nj = f2 // 2 // tn
    return pl.pallas_call(
        _pair_kernel,
        grid=(layers, nj),
        in_specs=[pl.BlockSpec((None, k, tn), lambda l, j: (l, 0, j)),
                  pl.BlockSpec((None, k, tn), lambda l, j: (l, 0, nj + j))],
        out_specs=pl.BlockSpec((None, k, 2 * tn), lambda l, j: (l, 0, j)),
        out_shape=jax.ShapeDtypeStruct((layers, k, f2), BF16),
        compiler_params=_params(2),
        name="pair_gate_up",
    )(w, w)


def _ln_kernel(y_ref, g_ref, b_ref, of_ref, ob_ref):
    y = y_ref[...]
    mu = jnp.mean(y, axis=-1, keepdims=True)
    yc = y - mu
    var = jnp.mean(yc * yc, axis=-1, keepdims=True)
    out = yc * lax.rsqrt(var + LN_EPS) * g_ref[...] + b_ref[...]
    of_ref[...] = out
    ob_ref[...] = out.astype(BF16)


def _layer_norm(y, gain, bias):
    m, d = y.shape
    tm = _pick(m, (256, 128))
    return pl.pallas_call(
        _ln_kernel,
        grid=(m // tm,),
        in_specs=[pl.BlockSpec((tm, d), lambda i: (i, 0)),
                  pl.BlockSpec((1, d), lambda i: (0, 0)),
                  pl.BlockSpec((1, d), lambda i: (0, 0))],
        out_specs=[pl.BlockSpec((tm, d), lambda i: (i, 0)),
                   pl.BlockSpec((tm, d), lambda i: (i, 0))],
        out_shape=[jax.ShapeDtypeStruct((m, d), F32), jax.ShapeDtypeStruct((m, d), BF16)],
        compiler_params=_params(1),
        name="layer_norm",
    )(y, gain.reshape(1, d), bias.reshape(1, d))


def _gdn_gates_kernel(x_ref, w_ref, alog_ref, dtb_ref, gates_ref, gsuf_ref, *, heads, chunk):
    p = _dot(x_ref[...], w_ref[...])
    tm = p.shape[0]
    lane = lax.broadcasted_iota(jnp.int32, p.shape, 1)
    pos = lax.broadcasted_iota(jnp.int32, p.shape, 0) % chunk
    g = -jnp.exp(alog_ref[...]) * _softplus(p + dtb_ref[...])
    pre = g
    suf = g
    s = 1
    while s < chunk:
        pre = pre + jnp.where(pos >= s, pltpu.roll(pre, s, 0), 0.0)
        suf = suf + jnp.where(pos < chunk - s, pltpu.roll(suf, tm - s, 0), 0.0)
        s *= 2
    gates_ref[...] = jnp.where(lane < heads, jax.nn.sigmoid(p), pre)
    gsuf_ref[...] = suf - g


def _gdn_gates(xb, w_ba, a_log, dt_bias, heads):
    m, k = xb.shape
    tm = _pick(m, (512, 256, 128))
    pad = LANES - 2 * heads
    wp = jnp.pad(w_ba, ((0, 0), (0, pad)))
    alog = jnp.pad(a_log.reshape(1, heads), ((0, 0), (heads, pad)))
    dtb = jnp.pad(dt_bias.reshape(1, heads), ((0, 0), (heads, pad)))
    kern = functools.partial(_gdn_gates_kernel, heads=heads, chunk=GDN_CHUNK)
    return pl.pallas_call(
        kern,
        grid=(m // tm,),
        in_specs=[pl.BlockSpec((tm, k), lambda i: (i, 0)),
                  pl.BlockSpec((k, LANES), lambda i: (0, 0)),
                  pl.BlockSpec((1, LANES), lambda i: (0, 0)),
                  pl.BlockSpec((1, LANES), lambda i: (0, 0))],
        out_specs=[pl.BlockSpec((tm, LANES), lambda i: (i, 0)),
                   pl.BlockSpec((tm, LANES), lambda i: (i, 0))],
        out_shape=[jax.ShapeDtypeStruct((m, LANES), F32), jax.ShapeDtypeStruct((m, LANES), F32)],
        compiler_params=_params(1),
        name="gdn_gates",
    )(xb, wp, alog, dtb)


def _gdn_kernel(q_ref, k_ref, v_ref, z_ref, gates_ref, gsuf_ref, gcrow_ref, nw_ref, o_ref, s_ref, *,
                heads, chunk, nh):
    hg = pl.program_id(1)
    t = pl.program_id(2)

    @pl.when(t == 0)
    def _():
        s_ref[...] = jnp.zeros(s_ref.shape, F32)

    tc = q_ref.shape[0]
    dk = HEAD_DIM
    hs = range(nh)
    shift = int(math.log2(chunk))
    lane = lax.broadcasted_iota(jnp.int32, (tc, LANES), 1)
    rows = lax.broadcasted_iota(jnp.int32, (tc, tc), 0)
    cols = lax.broadcasted_iota(jnp.int32, (tc, tc), 1)
    same = (rows >> shift) == (cols >> shift)
    incl = same & (rows >= cols)
    strict = same & (rows > cols)
    eye = jnp.where(rows == cols, 1.0, 0.0)
    gts = gates_ref[...]
    gsf = gsuf_ref[...]

    def col(x, idx):
        return jnp.sum(jnp.where(lane == idx, x, 0.0), axis=1, keepdims=True)

    sl = [slice(h * dk, (h + 1) * dk) for h in hs]
    beta = [col(gts, hg * nh + h) for h in hs]
    gc = [col(gts, heads + hg * nh + h) for h in hs]
    gs = [col(gsf, heads + hg * nh + h) for h in hs]
    q = [q_ref[:, sl[h]] for h in hs]
    k = [k_ref[:, sl[h]] for h in hs]
    v = [v_ref[:, sl[h]] for h in hs]
    qn = [q[h] * lax.rsqrt(jnp.sum(q[h] * q[h], axis=-1, keepdims=True) + RMS_EPS) * (dk ** -0.5) for h in hs]
    kn = [k[h] * lax.rsqrt(jnp.sum(k[h] * k[h], axis=-1, keepdims=True) + RMS_EPS) for h in hs]
    kb = [kn[h] * beta[h] for h in hs]
    decay = [jnp.where(incl, jnp.exp(jnp.where(incl, gc[h] - gcrow_ref[h], 0.0)), 0.0) for h in hs]
    kn16 = [kn[h].astype(BF16) for h in hs]
    a = [jnp.where(strict, _dot_nt(kb[h].astype(BF16), kn16[h]) * decay[h], 0.0) for h in hs]
    qk = [(_dot_nt(qn[h].astype(BF16), kn16[h]) * decay[h]).astype(BF16) for h in hs]

    p = [eye - a[h] for h in hs]
    ak = [a[h].astype(BF16) for h in hs]
    for _ in range(shift - 1):
        ak = [_dot(ak[h], ak[h]).astype(BF16) for h in hs]
        p = [p[h] + _dot(p[h].astype(BF16), ak[h]) for h in hs]

    egc = [jnp.exp(gc[h]) for h in hs]
    rhs = [jnp.concatenate([v[h] * beta[h], kb[h] * egc[h]], axis=1).astype(BF16) for h in hs]
    sol = [_dot(p[h].astype(BF16), rhs[h]) for h in hs]
    u = [sol[h][:, :dk] for h in hs]
    w = [sol[h][:, dk:] for h in hs]
    q_dec = [qn[h] * egc[h] for h in hs]
    k_tail = [kn[h] * jnp.exp(gs[h]) for h in hs]

    state = [s_ref[h] for h in hs]
    outs = [[] for _ in hs]
    for c in range(tc // chunk):
        lo, hi = c * chunk, (c + 1) * chunk
        lhs = [jnp.concatenate([w[h][lo:hi], q_dec[h][lo:hi]], axis=0).astype(BF16) for h in hs]
        r = [_dot(lhs[h], state[h].astype(BF16)) for h in hs]
        v_new = [u[h][lo:hi] - r[h][:chunk] for h in hs]
        v_pad = []
        for h in hs:
            pieces = ([jnp.zeros((lo, dk), F32)] if lo else []) + [v_new[h]]
            pieces += [jnp.zeros((tc - hi, dk), F32)] if tc - hi else []
            v_pad.append(jnp.concatenate(pieces, axis=0).astype(BF16) if len(pieces) > 1
                         else v_new[h].astype(BF16))
        for h in hs:
            outs[h].append(r[h][chunk:] + _dot(qk[h][lo:hi], v_pad[h]))
        state = [state[h] * jnp.exp(gc[h][hi - 1:hi, :])
                 + _dot(k_tail[h][lo:hi].T.astype(BF16), v_new[h].astype(BF16)) for h in hs]
    for h in hs:
        s_ref[h] = state[h]
        o = jnp.concatenate(outs[h], axis=0)
        o = o * lax.rsqrt(jnp.mean(o * o, axis=-1, keepdims=True) + RMS_EPS) * nw_ref[...]
        zz = z_ref[:, sl[h]]
        o_ref[:, sl[h]] = (o * (zz * jax.nn.sigmoid(zz))).astype(o_ref.dtype)


def _gdn_core(qkv, z, gates, gsuf, norm_w, *, batch, seq_len, heads):
    m = qkv.shape[0]
    tc = _pick(seq_len, (256, 128, 64))
    nt = seq_len // tc
    nh = math.gcd(heads, GDN_HEADS_PER_STEP)
    ng = heads // nh
    gc_rows = gates[:, heads:2 * heads].T.reshape(ng, nh, 1, m)
    kern = functools.partial(_gdn_kernel, heads=heads, chunk=GDN_CHUNK, nh=nh)

    def tok(col):
        return pl.BlockSpec((tc, nh * HEAD_DIM), lambda b, h, t: (b * nt + t, col * ng + h))

    return pl.pallas_call(
        kern,
        grid=(batch, ng, nt),
        in_specs=[tok(0), tok(1), tok(2),
                  pl.BlockSpec((tc, nh * HEAD_DIM), lambda b, h, t: (b * nt + t, h)),
                  pl.BlockSpec((tc, LANES), lambda b, h, t: (b * nt + t, 0)),
                  pl.BlockSpec((tc, LANES), lambda b, h, t: (b * nt + t, 0)),
                  pl.BlockSpec((None, nh, 1, tc), lambda b, h, t: (h, 0, 0, b * nt + t)),
                  pl.BlockSpec((1, HEAD_DIM), lambda b, h, t: (0, 0))],
        out_specs=pl.BlockSpec((tc, nh * HEAD_DIM), lambda b, h, t: (b * nt + t, h)),
        out_shape=jax.ShapeDtypeStruct((m, heads * HEAD_DIM), BF16),
        scratch_shapes=[pltpu.VMEM((nh, HEAD_DIM, HEAD_DIM), F32)],
        compiler_params=_params(3),
        name="gdn_core",
    )(qkv, qkv, qkv, z, gates, gsuf, gc_rows, norm_w.reshape(1, HEAD_DIM))


def _gdn_mixer(xb, w16, layer, conv_w, a_log, dt_bias, norm_w, *, batch, seq_len):
    d = xb.shape[1]
    heads = d // HEAD_DIM
    qkv = _proj(xb, w16, layer, col0=0, ncols=3 * d, seq_len=seq_len, out_dtype=F32, cw=conv_w, act="silu",
                name="gdn_qkv_proj")
    z = _proj(xb, w16, layer, col0=3 * d, ncols=d, seq_len=seq_len, out_dtype=F32, name="gdn_z_proj")
    gates, gsuf = _gdn_gates(xb, w16[layer, :, 4 * d:], a_log, dt_bias, heads)
    return _gdn_core(qkv, z, gates, gsuf, norm_w, batch=batch, seq_len=seq_len, heads=heads)


def _sb_kernel(q_ref, k_ref, v_ref, o_ref, *, blk, nh):
    i = pl.program_id(2)
    hs = range(nh)
    sl = [slice(h * HEAD_DIM, (h + 1) * HEAD_DIM) for h in hs]
    q = [q_ref[:, sl[h]] for h in hs]
    rows = lax.broadcasted_iota(jnp.int32, (blk, blk), 0)
    cols = lax.broadcasted_iota(jnp.int32, (blk, blk), 1)
    from_key = jnp.where(rows >= cols, 1.0, 0.0).astype(BF16)
    before = cols < rows

    def scores(j):
        start = pl.multiple_of(j * blk, blk)
        return [_dot_nt(q[h], k_ref[pl.ds(start, blk), sl[h]]) for h in hs]

    def block(j, nz, accs, runs, masked):
        start = pl.multiple_of(j * blk, blk)
        log_keep = [jnp.minimum(nz[h], 0.0) - jnp.log(1.0 + jnp.exp2(jnp.abs(nz[h]) * -LOG2_E)) for h in hs]
        if masked:
            log_keep = [jnp.where(before, log_keep[h], 0.0) for h in hs]
        keep_from = [_dot(log_keep[h].astype(BF16), from_key) for h in hs]
        wgt = [jnp.exp(keep_from[h] - nz[h] + runs[h]) for h in hs]
        if masked:
            wgt = [jnp.where(before, wgt[h], 0.0) for h in hs]
        accs = [accs[h] + _dot(wgt[h].astype(BF16), v_ref[pl.ds(start, blk), sl[h]]) for h in hs]
        runs = [runs[h] + keep_from[h][:, 0:1] for h in hs]
        return accs, runs

    accs, runs = block(i, scores(i), [jnp.zeros((blk, HEAD_DIM), F32) for _ in hs],
                       [jnp.zeros((blk, 1), F32) for _ in hs], True)

    def body(jj, carry):
        j = i - 1 - jj
        return block(j, scores(j), carry[0], carry[1], False)

    accs, _ = lax.fori_loop(0, i, body, (accs, runs))
    for h in hs:
        o_ref[:, sl[h]] = accs[h].astype(o_ref.dtype)


def _sb_attention(qkv, *, batch, seq_len, heads):
    m = qkv.shape[0]
    blk = _pick(seq_len, (256, 128))
    nq = seq_len // blk
    nh = math.gcd(heads, SB_HEADS_PER_STEP)
    ng = heads // nh
    kern = functools.partial(_sb_kernel, blk=blk, nh=nh)
    return pl.pallas_call(
        kern,
        grid=(batch, ng, nq),
        in_specs=[pl.BlockSpec((blk, nh * HEAD_DIM), lambda b, h, i: (b * nq + i, h)),
                  pl.BlockSpec((seq_len, nh * HEAD_DIM), lambda b, h, i: (b, ng + h)),
                  pl.BlockSpec((seq_len, nh * HEAD_DIM), lambda b, h, i: (b, 2 * ng + h))],
        out_specs=pl.BlockSpec((blk, nh * HEAD_DIM), lambda b, h, i: (b * nq + i, h)),
        out_shape=jax.ShapeDtypeStruct((m, heads * HEAD_DIM), BF16),
        compiler_params=_params(3),
        name="sb_attention",
    )(qkv, qkv, qkv)


def _sb_mixer(xb, w16, layer, *, batch, seq_len):
    d = xb.shape[1]
    qkv = _proj(xb, w16, layer, col0=0, ncols=3 * d, seq_len=seq_len, out_dtype=BF16, name="sb_qkv_proj",
                lead_scale=(d, -(HEAD_DIM ** -0.5)))
    return _sb_attention(qkv, batch=batch, seq_len=seq_len, heads=d // HEAD_DIM)


def _lru_kernel(xr_ref, y_ref, wg_ref, bg_ref, lam_ref, o_ref, h_ref, a_scr, u_scr):
    t = pl.program_id(2)

    @pl.when(t == 0)
    def _():
        h_ref[...] = jnp.zeros(h_ref.shape, F32)

    x = xr_ref[...]
    tt, c = x.shape
    gates = jax.nn.sigmoid(_dot(x.astype(BF16), wg_ref[...]) + bg_ref[...])
    log_a = -LRU_C * gates[:, :c] * _softplus(-lam_ref[...])
    a = jnp.exp(log_a)
    a_scr[...] = a
    u_scr[...] = jnp.sqrt(-jnp.tanh(log_a) * (a * a + 1.0)) * gates[:, c:] * x

    pos = lax.broadcasted_iota(jnp.int32, (SUBLANES, c), 0)

    def slab(s, h):
        r0 = pl.multiple_of(s * SUBLANES, SUBLANES)
        a8 = a_scr[pl.ds(r0, SUBLANES), :]
        u8 = u_scr[pl.ds(r0, SUBLANES), :]
        d = 1
        while d < SUBLANES:
            a_prev = jnp.where(pos >= d, pltpu.roll(a8, d, 0), 1.0)
            u_prev = jnp.where(pos >= d, pltpu.roll(u8, d, 0), 0.0)
            u8 = a8 * u_prev + u8
            a8 = a8 * a_prev
            d *= 2
        h8 = a8 * h + u8
        u_scr[pl.ds(r0, SUBLANES), :] = h8
        return h8[SUBLANES - 1:SUBLANES, :]

    h_ref[...] = lax.fori_loop(0, tt // SUBLANES, slab, h_ref[...])
    o_ref[...] = (u_scr[...] * y_ref[...]).astype(o_ref.dtype)


def _lru_core(xr, y, w_gate_a, b_gate_a, w_gate_x, b_gate_x, lam, *, batch, seq_len):
    m, d = xr.shape
    c = LRU_BLOCK_DIM
    nb = d // c
    tt = _pick(seq_len, (512, 256, 128))
    nt = seq_len // tt
    w_gates = jnp.concatenate([w_gate_a, w_gate_x], axis=-1).astype(BF16)
    b_gates = jnp.concatenate([b_gate_a, b_gate_x], axis=-1).reshape(nb, 1, 2 * c)

    def tok():
        return pl.BlockSpec((tt, c), lambda b, n, t: (b * nt + t, n))

    def per_block(rows, width):
        return pl.BlockSpec((None, rows, width), lambda b, n, t: (n, 0, 0))

    return pl.pallas_call(
        _lru_kernel,
        grid=(batch, nb, nt),
        in_specs=[tok(), tok(), per_block(c, 2 * c), per_block(1, 2 * c), per_block(1, c)],
        out_specs=tok(),
        out_shape=jax.ShapeDtypeStruct((m, d), BF16),
        scratch_shapes=[pltpu.VMEM((1, c), F32), pltpu.VMEM((tt, c), F32), pltpu.VMEM((tt, c), F32)],
        compiler_params=_params(3),
        name="lru_core",
    )(xr, y, w_gates, b_gates, lam.reshape(nb, 1, c))


def _lru_mixer(xb, w16, layer, conv_w, conv_b, w_gate_a, b_gate_a, w_gate_x, b_gate_x, lam, *, batch, seq_len):
    d = xb.shape[1]
    xr = _proj(xb, w16, layer, col0=0, ncols=d, seq_len=seq_len, out_dtype=F32, cw=conv_w, bias=conv_b,
               name="lru_x_proj")
    y = _proj(xb, w16, layer, col0=d, ncols=d, seq_len=seq_len, out_dtype=F32, act="gelu", name="lru_y_proj")
    return _lru_core(xr, y, w_gate_a, b_gate_a, w_gate_x, b_gate_x, lam, batch=batch, seq_len=seq_len)


def kernel(x, ln_gain, ln_bias, gdn_w_in, gdn_conv_w, gdn_a_log, gdn_dt_bias, gdn_norm_w, gdn_w_out, sb_w_in, sb_w_out, lru_w_in, lru_conv_w, lru_conv_b, lru_w_gate_a, lru_b_gate_a, lru_w_gate_x, lru_b_gate_x, lru_lambda, lru_w_out, ffn_w_up, ffn_conv_w, ffn_conv_b, ffn_w_down):
    batch, seq_len, d = x.shape
    depth = ln_gain.shape[0]
    d_ff = ffn_w_down.shape[1]
    alpha = (2 * depth) ** 0.25
    xf = x.reshape(batch * seq_len, d)
    xb = xf.astype(BF16)
    gdn_in, gdn_out = gdn_w_in.astype(BF16), gdn_w_out.astype(BF16)
    sb_in, sb_out = sb_w_in.astype(BF16), sb_w_out.astype(BF16)
    lru_in, lru_out = lru_w_in.astype(BF16), lru_w_out.astype(BF16)
    ffn_up = _pair_columns(ffn_w_up, MXU_PAIR_COLS // 2)
    ffn_down = ffn_w_down.astype(BF16)
    for i in range(depth):
        kind, j = i % 3, i // 3
        if kind == 0:
            mix = _gdn_mixer(xb, gdn_in, j, gdn_conv_w[j], gdn_a_log[j], gdn_dt_bias[j], gdn_norm_w[j],
                             batch=batch, seq_len=seq_len)
            w_out = gdn_out
        elif kind == 1:
            mix = _sb_mixer(xb, sb_in, j, batch=batch, seq_len=seq_len)
            w_out = sb_out
        else:
            mix = _lru_mixer(xb, lru_in, j, lru_conv_w[j], lru_conv_b[j], lru_w_gate_a[j], lru_b_gate_a[j],
                             lru_w_gate_x[j], lru_b_gate_x[j], lru_lambda[j], batch=batch, seq_len=seq_len)
            w_out = lru_out
        y = _proj(mix, w_out, j, col0=0, ncols=d, seq_len=seq_len, out_dtype=F32, resid=xf, alpha=alpha,
                  name="mixer_out_proj")
        xf, xb = _layer_norm(y, ln_gain[i, 0], ln_bias[i, 0])
        hid = _proj(xb, ffn_up, i, col0=0, ncols=d_ff, seq_len=seq_len, out_dtype=BF16, cw=ffn_conv_w[i],
                    bias=ffn_conv_b[i], act="silu", gated=True, name="ffn_up")
        y = _proj(hid, ffn_down, i, col0=0, ncols=d, seq_len=seq_len, out_dtype=F32, resid=xf, alpha=alpha,
                  name="ffn_down")
        xf, xb = _layer_norm(y, ln_gain[i, 1], ln_bias[i, 1])
    return xf.reshape(batch, seq_len, d)
```

```python
import functools
import math

import jax
import jax.numpy as jnp
from jax import lax
from jax.experimental import pallas as pl
from jax.experimental.pallas import tpu as pltpu

F32 = jnp.float32
BF16 = jnp.bfloat16

LN_EPS = 1e-5
RMS_EPS = 1e-6
HEAD_DIM = 128
GDN_CHUNK = 64
GDN_HEADS_PER_STEP = 8
SB_HEADS_PER_STEP = 4
LRU_BLOCK_DIM = 256
LRU_C = 8.0
SUBLANES = 8
LANES = 128
LOG2_E = 1.4426950408889634
MXU_PAIR_COLS = 512
VMEM_LIMIT_BYTES = 56 * 1024 * 1024
LHS_TILE_BYTES = 12 * 1024 * 1024


def _pick(n, candidates):
    for c in candidates:
        if n % c == 0:
            return c
    raise ValueError(f"no tile in {candidates} divides {n}")


def _params(n_axes):
    return pltpu.CompilerParams(dimension_semantics=("arbitrary",) * n_axes,
                                vmem_limit_bytes=VMEM_LIMIT_BYTES)


def _softplus(x):
    return jnp.maximum(x, 0.0) + jnp.log1p(jnp.exp(-jnp.abs(x)))


def _dot(a, b):
    return jnp.dot(a, b, preferred_element_type=F32)


def _dot_nt(a, b):
    return lax.dot_general(a, b, (((1,), (1,)), ((), ())), preferred_element_type=F32)


def _proj_kernel(*refs, taps, has_bias, act, gated, alpha, lead, tm, tn, tiles_per_seq):
    it = iter(refs)
    x_ref = next(it)
    w_ref = next(it)
    cw_ref = next(it) if taps else None
    b_ref = next(it) if has_bias else None
    r_ref = next(it) if alpha is not None else None
    o_ref = next(it)
    ext_ref = next(it) if taps else None
    halo_ref = next(it) if taps else None

    i = pl.program_id(0)
    j = pl.program_id(1)
    if taps:
        seq_start = (i % tiles_per_seq) == 0

        @pl.when(seq_start)
        def _():
            ext_ref[0:SUBLANES, :] = jnp.zeros((SUBLANES, tn), F32)

        @pl.when(jnp.logical_not(seq_start))
        def _():
            ext_ref[0:SUBLANES, :] = halo_ref[j]

    acc = _dot(x_ref[...], w_ref[...])
    if gated:
        up = acc[:, tn:]
        acc = acc[:, :tn]
    if taps:
        ext_ref[SUBLANES:, :] = acc
        halo_ref[j] = acc[tm - SUBLANES:, :]
        out = acc * cw_ref[taps - 1:taps, :]
        for s in range(1, taps):
            out = out + ext_ref[SUBLANES - s:SUBLANES - s + tm, :] * cw_ref[taps - 1 - s:taps - s, :]
    else:
        out = acc
    if has_bias:
        out = out + b_ref[...]
    if act == "silu":
        out = out * jax.nn.sigmoid(out)
    elif act == "gelu":
        out = jax.nn.gelu(out, approximate=True)
    if gated:
        out = out * up
    if alpha is not None:
        out = alpha * r_ref[...] + out
    if lead is not None:
        out = out * jnp.where(j < lead[0], lead[1], 1.0)
    o_ref[...] = out.astype(o_ref.dtype)


def _proj(xb, w, layer, *, col0, ncols, seq_len, out_dtype, name, cw=None, bias=None, act="none", gated=False,
          resid=None, alpha=None, lead_scale=None):
    m, k = xb.shape
    tn = _pick(ncols, (MXU_PAIR_COLS // 2,) if gated else (MXU_PAIR_COLS, 256, 128))
    wn = 2 * tn if gated else tn
    tm = _pick(seq_len, tuple(t for t in (1024, 512, 256, 128) if t * k * 2 <= LHS_TILE_BYTES) or (128,))
    taps = 0 if cw is None else cw.shape[0]
    nj = ncols // tn
    c0 = col0 // tn
    assert col0 % tn == 0 and m % seq_len == 0
    in_specs = [pl.BlockSpec((tm, k), lambda i, j: (i, 0)),
                pl.BlockSpec((None, k, wn), lambda i, j: (layer, 0, c0 + j))]
    args = [xb, w]
    scratch = []
    if taps:
        in_specs.append(pl.BlockSpec((taps, tn), lambda i, j: (0, j)))
        args.append(cw)
        scratch = [pltpu.VMEM((tm + SUBLANES, tn), F32), pltpu.VMEM((nj, SUBLANES, tn), F32)]
    if bias is not None:
        in_specs.append(pl.BlockSpec((1, tn), lambda i, j: (0, j)))
        args.append(bias.reshape(1, ncols))
    if resid is not None:
        in_specs.append(pl.BlockSpec((tm, tn), lambda i, j: (i, j)))
        args.append(resid)
    lead = None
    if lead_scale is not None:
        assert lead_scale[0] % tn == 0
        lead = (lead_scale[0] // tn, lead_scale[1])
    kern = functools.partial(_proj_kernel, taps=taps, has_bias=bias is not None, act=act, gated=gated,
                             alpha=alpha, lead=lead, tm=tm, tn=tn, tiles_per_seq=seq_len // tm)
    return pl.pallas_call(
        kern,
        grid=(m // tm, nj),
        in_specs=in_specs,
        out_specs=pl.BlockSpec((tm, tn), lambda i, j: (i, j)),
        out_shape=jax.ShapeDtypeStruct((m, ncols), out_dtype),
        scratch_shapes=scratch,
        compiler_params=_params(2),
        name=name,
    )(*args)


def _pair_kernel(g_ref, u_ref, o_ref):
    tn = g_ref.shape[1]
    o_ref[:, :tn] = g_ref[...].astype(BF16)
    o_ref[:, tn:] = u_ref[...].astype(BF16)


def _pair_columns(w, tn):
    layers, k, f2 = w.shape
    nj = f2 // 2 // tn
    return pl.pallas_call(
        _pair_kernel,
        grid=(layers, nj),
        in_specs=[pl.BlockSpec((None, k, tn), lambda l, j: (l, 0, j)),
                  pl.BlockSpec((None, k, tn), lambda l, j: (l, 0, nj + j))],
        out_specs=pl.BlockSpec((None, k, 2 * tn), lambda l, j: (l, 0, j)),
        out_shape=jax.ShapeDtypeStruct((layers, k, f2), BF16),
        compiler_params=_params(2),
        name="pair_gate_up",
    )(w, w)


def _ln_kernel(y_ref, g_ref, b_ref, of_ref, ob_ref):
    y = y_ref[...]
    mu = jnp.mean(y, axis=-1, keepdims=True)
    yc = y - mu
    var = jnp.mean(yc * yc, axis=-1, keepdims=True)
    out = yc * lax.rsqrt(var + LN_EPS) * g_ref[...] + b_ref[...]
    of_ref[...] = out
    ob_ref[...] = out.astype(BF16)


def _layer_norm(y, gain, bias):
    m, d = y.shape
    tm = _pick(m, (256, 128))
    return pl.pallas_call(
        _ln_kernel,
        grid=(m // tm,),
        in_specs=[pl.BlockSpec((tm, d), lambda i: (i, 0)),
                  pl.BlockSpec((1, d), lambda i: (0, 0)),
                  pl.BlockSpec((1, d), lambda i: (0, 0))],
        out_specs=[pl.BlockSpec((tm, d), lambda i: (i, 0)),
                   pl.BlockSpec((tm, d), lambda i: (i, 0))],
        out_shape=[jax.ShapeDtypeStruct((m, d), F32), jax.ShapeDtypeStruct((m, d), BF16)],
        compiler_params=_params(1),
        name="layer_norm",
    )(y, gain.reshape(1, d), bias.reshape(1, d))


def _outproj_ln_kernel(a_ref, w_ref, x_ref, g_ref, b_ref, of_ref, ob_ref, y_scr, *, alpha, nj, tn):
    j = pl.program_id(1)
    y_scr[j] = alpha * x_ref[...] + _dot(a_ref[...], w_ref[...])

    @pl.when(j == nj - 1)
    def _():
        d = nj * tn
        total = y_scr[0].sum(axis=-1, keepdims=True)
        for c in range(1, nj):
            total = total + y_scr[c].sum(axis=-1, keepdims=True)
        mu = total / d
        sq = jnp.zeros_like(mu)
        for c in range(nj):
            yc = y_scr[c] - mu
            sq = sq + (yc * yc).sum(axis=-1, keepdims=True)
        rstd = lax.rsqrt(sq / d + LN_EPS)
        for c in range(nj):
            cols = slice(c * tn, (c + 1) * tn)
            out = (y_scr[c] - mu) * rstd * g_ref[:, cols] + b_ref[:, cols]
            of_ref[:, cols] = out
            ob_ref[:, cols] = out.astype(BF16)


def _outproj_ln(a, w, layer, x, gain, bias, alpha, *, seq_len):
    m, k = a.shape
    d = w.shape[2]
    tn = _pick(d, (MXU_PAIR_COLS, 256, 128))
    tm = _pick(seq_len, (512, 256, 128))
    nj = d // tn
    kern = functools.partial(_outproj_ln_kernel, alpha=alpha, nj=nj, tn=tn)
    return pl.pallas_call(
        kern,
        grid=(m // tm, nj),
        in_specs=[pl.BlockSpec((tm, k), lambda i, j: (i, 0)),
                  pl.BlockSpec((None, k, tn), lambda i, j: (layer, 0, j)),
                  pl.BlockSpec((tm, tn), lambda i, j: (i, j)),
                  pl.BlockSpec((1, d), lambda i, j: (0, 0)),
                  pl.BlockSpec((1, d), lambda i, j: (0, 0))],
        out_specs=[pl.BlockSpec((tm, d), lambda i, j: (i, 0), pipeline_mode=pl.Buffered(1)),
                   pl.BlockSpec((tm, d), lambda i, j: (i, 0), pipeline_mode=pl.Buffered(1))],
        out_shape=[jax.ShapeDtypeStruct((m, d), F32), jax.ShapeDtypeStruct((m, d), BF16)],
        scratch_shapes=[pltpu.VMEM((nj, tm, tn), F32)],
        compiler_params=_params(2),
        name="mixer_out_proj_ln",
    )(a, w, x, gain.reshape(1, d), bias.reshape(1, d))


def _gdn_gates_kernel(x_ref, w_ref, alog_ref, dtb_ref, gates_ref, gsuf_ref, *, heads, chunk):
    p = _dot(x_ref[...], w_ref[...])
    tm = p.shape[0]
    lane = lax.broadcasted_iota(jnp.int32, p.shape, 1)
    pos = lax.broadcasted_iota(jnp.int32, p.shape, 0) % chunk
    g = -jnp.exp(alog_ref[...]) * _softplus(p + dtb_ref[...])
    pre = g
    suf = g
    s = 1
    while s < chunk:
        pre = pre + jnp.where(pos >= s, pltpu.roll(pre, s, 0), 0.0)
        suf = suf + jnp.where(pos < chunk - s, pltpu.roll(suf, tm - s, 0), 0.0)
        s *= 2
    gates_ref[...] = jnp.where(lane < heads, jax.nn.sigmoid(p), pre)
    gsuf_ref[...] = suf - g


def _gdn_gates(xb, w_ba, a_log, dt_bias, heads):
    m, k = xb.shape
    tm = _pick(m, (512, 256, 128))
    pad = LANES - 2 * heads
    wp = jnp.pad(w_ba, ((0, 0), (0, pad)))
    alog = jnp.pad(a_log.reshape(1, heads), ((0, 0), (heads, pad)))
    dtb = jnp.pad(dt_bias.reshape(1, heads), ((0, 0), (heads, pad)))
    kern = functools.partial(_gdn_gates_kernel, heads=heads, chunk=GDN_CHUNK)
    return pl.pallas_call(
        kern,
        grid=(m // tm,),
        in_specs=[pl.BlockSpec((tm, k), lambda i: (i, 0)),
                  pl.BlockSpec((k, LANES), lambda i: (0, 0)),
                  pl.BlockSpec((1, LANES), lambda i: (0, 0)),
                  pl.BlockSpec((1, LANES), lambda i: (0, 0))],
        out_specs=[pl.BlockSpec((tm, LANES), lambda i: (i, 0)),
                   pl.BlockSpec((tm, LANES), lambda i: (i, 0))],
        out_shape=[jax.ShapeDtypeStruct((m, LANES), F32), jax.ShapeDtypeStruct((m, LANES), F32)],
        compiler_params=_params(1),
        name="gdn_gates",
    )(xb, wp, alog, dtb)


def _gdn_kernel(q_ref, k_ref, v_ref, z_ref, gates_ref, gsuf_ref, gcrow_ref, nw_ref, o_ref, s_ref, *,
                heads, chunk, nh):
    hg = pl.program_id(1)
    t = pl.program_id(2)

    @pl.when(t == 0)
    def _():
        s_ref[...] = jnp.zeros(s_ref.shape, F32)

    tc = q_ref.shape[0]
    dk = HEAD_DIM
    hs = range(nh)
    shift = int(math.log2(chunk))
    lane = lax.broadcasted_iota(jnp.int32, (tc, LANES), 1)
    rows = lax.broadcasted_iota(jnp.int32, (tc, tc), 0)
    cols = lax.broadcasted_iota(jnp.int32, (tc, tc), 1)
    same = (rows >> shift) == (cols >> shift)
    incl = same & (rows >= cols)
    strict = same & (rows > cols)
    eye = jnp.where(rows == cols, 1.0, 0.0)
    gts = gates_ref[...]
    gsf = gsuf_ref[...]

    def col(x, idx):
        return jnp.sum(jnp.where(lane == idx, x, 0.0), axis=1, keepdims=True)

    sl = [slice(h * dk, (h + 1) * dk) for h in hs]
    beta = [col(gts, hg * nh + h) for h in hs]
    gc = [col(gts, heads + hg * nh + h) for h in hs]
    gs = [col(gsf, heads + hg * nh + h) for h in hs]
    q = [q_ref[:, sl[h]] for h in hs]
    k = [k_ref[:, sl[h]] for h in hs]
    v = [v_ref[:, sl[h]] for h in hs]
    qn = [q[h] * lax.rsqrt(jnp.sum(q[h] * q[h], axis=-1, keepdims=True) + RMS_EPS) * (dk ** -0.5) for h in hs]
    kn = [k[h] * lax.rsqrt(jnp.sum(k[h] * k[h], axis=-1, keepdims=True) + RMS_EPS) for h in hs]
    kb = [kn[h] * beta[h] for h in hs]
    decay = [jnp.where(incl, jnp.exp(jnp.where(incl, gc[h] - gcrow_ref[h], 0.0)), 0.0) for h in hs]
    kn16 = [kn[h].astype(BF16) for h in hs]
    a = [jnp.where(strict, _dot_nt(kb[h].astype(BF16), kn16[h]) * decay[h], 0.0) for h in hs]
    qk = [(_dot_nt(qn[h].astype(BF16), kn16[h]) * decay[h]).astype(BF16) for h in hs]

    p = [eye - a[h] for h in hs]
    ak = [a[h].astype(BF16) for h in hs]
    for _ in range(shift - 1):
        ak = [_dot(ak[h], ak[h]).astype(BF16) for h in hs]
        p = [p[h] + _dot(p[h].astype(BF16), ak[h]) for h in hs]

    egc = [jnp.exp(gc[h]) for h in hs]
    rhs = [jnp.concatenate([v[h] * beta[h], kb[h] * egc[h]], axis=1).astype(BF16) for h in hs]
    sol = [_dot(p[h].astype(BF16), rhs[h]) for h in hs]
    u = [sol[h][:, :dk] for h in hs]
    w = [sol[h][:, dk:] for h in hs]
    q_dec = [qn[h] * egc[h] for h in hs]
    k_tail = [kn[h] * jnp.exp(gs[h]) for h in hs]

    state = [s_ref[h] for h in hs]
    outs = [[] for _ in hs]
    for c in range(tc // chunk):
        lo, hi = c * chunk, (c + 1) * chunk
        lhs = [jnp.concatenate([w[h][lo:hi], q_dec[h][lo:hi]], axis=0).astype(BF16) for h in hs]
        r = [_dot(lhs[h], state[h].astype(BF16)) for h in hs]
        v_new = [u[h][lo:hi] - r[h][:chunk] for h in hs]
        v_pad = []
        for h in hs:
            pieces = ([jnp.zeros((lo, dk), F32)] if lo else []) + [v_new[h]]
            pieces += [jnp.zeros((tc - hi, dk), F32)] if tc - hi else []
            v_pad.append(jnp.concatenate(pieces, axis=0).astype(BF16) if len(pieces) > 1
                         else v_new[h].astype(BF16))
        for h in hs:
            outs[h].append(r[h][chunk:] + _dot(qk[h][lo:hi], v_pad[h]))
        state = [state[h] * jnp.exp(gc[h][hi - 1:hi, :])
                 + _dot(k_tail[h][lo:hi].T.astype(BF16), v_new[h].astype(BF16)) for h in hs]
    for h in hs:
        s_ref[h] = state[h]
        o = jnp.concatenate(outs[h], axis=0)
        o = o * lax.rsqrt(jnp.mean(o * o, axis=-1, keepdims=True) + RMS_EPS) * nw_ref[...]
        zz = z_ref[:, sl[h]]
        o_ref[:, sl[h]] = (o * (zz * jax.nn.sigmoid(zz))).astype(o_ref.dtype)


def _gdn_core(qkv, z, gates, gsuf, norm_w, *, batch, seq_len, heads):
    m = qkv.shape[0]
    tc = _pick(seq_len, (256, 128, 64))
    nt = seq_len // tc
    nh = math.gcd(heads, GDN_HEADS_PER_STEP)
    ng = heads // nh
    gc_rows = gates[:, heads:2 * heads].T.reshape(ng, nh, 1, m)
    kern = functools.partial(_gdn_kernel, heads=heads, chunk=GDN_CHUNK, nh=nh)

    def tok(col):
        return pl.BlockSpec((tc, nh * HEAD_DIM), lambda b, h, t: (b * nt + t, col * ng + h))

    return pl.pallas_call(
        kern,
        grid=(batch, ng, nt),
        in_specs=[tok(0), tok(1), tok(2),
                  pl.BlockSpec((tc, nh * HEAD_DIM), lambda b, h, t: (b * nt + t, h)),
                  pl.BlockSpec((tc, LANES), lambda b, h, t: (b * nt + t, 0)),
                  pl.BlockSpec((tc, LANES), lambda b, h, t: (b * nt + t, 0)),
                  pl.BlockSpec((None, nh, 1, tc), lambda b, h, t: (h, 0, 0, b * nt + t)),
                  pl.BlockSpec((1, HEAD_DIM), lambda b, h, t: (0, 0))],
        out_specs=pl.BlockSpec((tc, nh * HEAD_DIM), lambda b, h, t: (b * nt + t, h)),
        out_shape=jax.ShapeDtypeStruct((m, heads * HEAD_DIM), BF16),
        scratch_shapes=[pltpu.VMEM((nh, HEAD_DIM, HEAD_DIM), F32)],
        compiler_params=_params(3),
        name="gdn_core",
    )(qkv, qkv, qkv, z, gates, gsuf, gc_rows, norm_w.reshape(1, HEAD_DIM))


def _gdn_mixer(xb, w16, layer, conv_w, a_log, dt_bias, norm_w, *, batch, seq_len):
    d = xb.shape[1]
    heads = d // HEAD_DIM
    qkv = _proj(xb, w16, layer, col0=0, ncols=3 * d, seq_len=seq_len, out_dtype=F32, cw=conv_w, act="silu",
                name="gdn_qkv_proj")
    z = _proj(xb, w16, layer, col0=3 * d, ncols=d, seq_len=seq_len, out_dtype=F32, name="gdn_z_proj")
    gates, gsuf = _gdn_gates(xb, w16[layer, :, 4 * d:], a_log, dt_bias, heads)
    return _gdn_core(qkv, z, gates, gsuf, norm_w, batch=batch, seq_len=seq_len, heads=heads)


def _sb_kernel(q_ref, k_ref, v_ref, o_ref, *, blk, nh):
    i = pl.program_id(2)
    hs = range(nh)
    sl = [slice(h * HEAD_DIM, (h + 1) * HEAD_DIM) for h in hs]
    q = [q_ref[:, sl[h]] for h in hs]
    rows = lax.broadcasted_iota(jnp.int32, (blk, blk), 0)
    cols = lax.broadcasted_iota(jnp.int32, (blk, blk), 1)
    from_key = jnp.where(rows >= cols, 1.0, 0.0).astype(BF16)
    before = cols < rows

    def scores(j):
        start = pl.multiple_of(j * blk, blk)
        return [_dot_nt(q[h], k_ref[pl.ds(start, blk), sl[h]]) for h in hs]

    def block(j, nz, accs, runs, masked):
        start = pl.multiple_of(j * blk, blk)
        log_keep = [jnp.minimum(nz[h], 0.0) - jnp.log(1.0 + jnp.exp2(jnp.abs(nz[h]) * -LOG2_E)) for h in hs]
        if masked:
            log_keep = [jnp.where(before, log_keep[h], 0.0) for h in hs]
        keep_from = [_dot(log_keep[h].astype(BF16), from_key) for h in hs]
        wgt = [jnp.exp(keep_from[h] - nz[h] + runs[h]) for h in hs]
        if masked:
            wgt = [jnp.where(before, wgt[h], 0.0) for h in hs]
        accs = [accs[h] + _dot(wgt[h].astype(BF16), v_ref[pl.ds(start, blk), sl[h]]) for h in hs]
        runs = [runs[h] + keep_from[h][:, 0:1] for h in hs]
        return accs, runs

    accs, runs = block(i, scores(i), [jnp.zeros((blk, HEAD_DIM), F32) for _ in hs],
                       [jnp.zeros((blk, 1), F32) for _ in hs], True)

    def body(jj, carry):
        j = i - 1 - jj
        return block(j, scores(j), carry[0], carry[1], False)

    accs, _ = lax.fori_loop(0, i, body, (accs, runs))
    for h in hs:
        o_ref[:, sl[h]] = accs[h].astype(o_ref.dtype)


def _sb_attention(qkv, *, batch, seq_len, heads):
    m = qkv.shape[0]
    blk = _pick(seq_len, (256, 128))
    nq = seq_len // blk
    nh = math.gcd(heads, SB_HEADS_PER_STEP)
    ng = heads // nh
    kern = functools.partial(_sb_kernel, blk=blk, nh=nh)
    return pl.pallas_call(
        kern,
        grid=(batch, ng, nq),
        in_specs=[pl.BlockSpec((blk, nh * HEAD_DIM), lambda b, h, i: (b * nq + i, h)),
                  pl.BlockSpec((seq_len, nh * HEAD_DIM), lambda b, h, i: (b, ng + h)),
                  pl.BlockSpec((seq_len, nh * HEAD_DIM), lambda b, h, i: (b, 2 * ng + h))],
        out_specs=pl.BlockSpec((blk, nh * HEAD_DIM), lambda b, h, i: (b * nq + i, h)),
        out_shape=jax.ShapeDtypeStruct((m, heads * HEAD_DIM), BF16),
        compiler_params=_params(3),
        name="sb_attention",
    )(qkv, qkv, qkv)


def _sb_mixer(xb, w16, layer, *, batch, seq_len):
    d = xb.shape[1]
    qkv = _proj(xb, w16, layer, col0=0, ncols=3 * d, seq_len=seq_len, out_dtype=BF16, name="sb_qkv_proj",
                lead_scale=(d, -(HEAD_DIM ** -0.5)))
    return _sb_attention(qkv, batch=batch, seq_len=seq_len, heads=d // HEAD_DIM)


def _lru_kernel(xr_ref, y_ref, wg_ref, bg_ref, lam_ref, o_ref, h_ref, a_scr, u_scr):
    t = pl.program_id(2)

    @pl.when(t == 0)
    def _():
        h_ref[...] = jnp.zeros(h_ref.shape, F32)

    x = xr_ref[...]
    tt, c = x.shape
    gates = jax.nn.sigmoid(_dot(x.astype(BF16), wg_ref[...]) + bg_ref[...])
    log_a = -LRU_C * gates[:, :c] * _softplus(-lam_ref[...])
    a = jnp.exp(log_a)
    a_scr[...] = a
    u_scr[...] = jnp.sqrt(-jnp.tanh(log_a) * (a * a + 1.0)) * gates[:, c:] * x

    pos = lax.broadcasted_iota(jnp.int32, (SUBLANES, c), 0)

    def slab(s, h):
        r0 = pl.multiple_of(s * SUBLANES, SUBLANES)
        a8 = a_scr[pl.ds(r0, SUBLANES), :]
        u8 = u_scr[pl.ds(r0, SUBLANES), :]
        d = 1
        while d < SUBLANES:
            a_prev = jnp.where(pos >= d, pltpu.roll(a8, d, 0), 1.0)
            u_prev = jnp.where(pos >= d, pltpu.roll(u8, d, 0), 0.0)
            u8 = a8 * u_prev + u8
            a8 = a8 * a_prev
            d *= 2
        h8 = a8 * h + u8
        u_scr[pl.ds(r0, SUBLANES), :] = h8
        return h8[SUBLANES - 1:SUBLANES, :]

    h_ref[...] = lax.fori_loop(0, tt // SUBLANES, slab, h_ref[...])
    o_ref[...] = (u_scr[...] * y_ref[...]).astype(o_ref.dtype)


def _lru_core(xr, y, w_gate_a, b_gate_a, w_gate_x, b_gate_x, lam, *, batch, seq_len):
    m, d = xr.shape
    c = LRU_BLOCK_DIM
    nb = d // c
    tt = _pick(seq_len, (512, 256, 128))
    nt = seq_len // tt
    w_gates = jnp.concatenate([w_gate_a, w_gate_x], axis=-1).astype(BF16)
    b_gates = jnp.concatenate([b_gate_a, b_gate_x], axis=-1).reshape(nb, 1, 2 * c)

    def tok():
        return pl.BlockSpec((tt, c), lambda b, n, t: (b * nt + t, n))

    def per_block(rows, width):
        return pl.BlockSpec((None, rows, width), lambda b, n, t: (n, 0, 0))

    return pl.pallas_call(
        _lru_kernel,
        grid=(batch, nb, nt),
        in_specs=[tok(), tok(), per_block(c, 2 * c), per_block(1, 2 * c), per_block(1, c)],
        out_specs=tok(),
        out_shape=jax.ShapeDtypeStruct((m, d), BF16),
        scratch_shapes=[pltpu.VMEM((1, c), F32), pltpu.VMEM((tt, c), F32), pltpu.VMEM((tt, c), F32)],
        compiler_params=_params(3),
        name="lru_core",
    )(xr, y, w_gates, b_gates, lam.reshape(nb, 1, c))


def _lru_mixer(xb, w16, layer, conv_w, conv_b, w_gate_a, b_gate_a, w_gate_x, b_gate_x, lam, *, batch, seq_len):
    d = xb.shape[1]
    xr = _proj(xb, w16, layer, col0=0, ncols=d, seq_len=seq_len, out_dtype=F32, cw=conv_w, bias=conv_b,
               name="lru_x_proj")
    y = _proj(xb, w16, layer, col0=d, ncols=d, seq_len=seq_len, out_dtype=F32, act="gelu", name="lru_y_proj")
    return _lru_core(xr, y, w_gate_a, b_gate_a, w_gate_x, b_gate_x, lam, batch=batch, seq_len=seq_len)


def kernel(x, ln_gain, ln_bias, gdn_w_in, gdn_conv_w, gdn_a_log, gdn_dt_bias, gdn_norm_w, gdn_w_out, sb_w_in, sb_w_out, lru_w_in, lru_conv_w, lru_conv_b, lru_w_gate_a, lru_b_gate_a, lru_w_gate_x, lru_b_gate_x, lru_lambda, lru_w_out, ffn_w_up, ffn_conv_w, ffn_conv_b, ffn_w_down):
    batch, seq_len, d = x.shape
    depth = ln_gain.shape[0]
    d_ff = ffn_w_down.shape[1]
    alpha = (2 * depth) ** 0.25
    xf = x.reshape(batch * seq_len, d)
    xb = xf.astype(BF16)
    gdn_in, gdn_out = gdn_w_in.astype(BF16), gdn_w_out.astype(BF16)
    sb_in, sb_out = sb_w_in.astype(BF16), sb_w_out.astype(BF16)
    lru_in, lru_out = lru_w_in.astype(BF16), lru_w_out.astype(BF16)
    ffn_up = _pair_columns(ffn_w_up, MXU_PAIR_COLS // 2)
    ffn_down = ffn_w_down.astype(BF16)
    for i in range(depth):
        kind, j = i % 3, i // 3
        if kind == 0:
            mix = _gdn_mixer(xb, gdn_in, j, gdn_conv_w[j], gdn_a_log[j], gdn_dt_bias[j], gdn_norm_w[j],
                             batch=batch, seq_len=seq_len)
            w_out = gdn_out
        elif kind == 1:
            mix = _sb_mixer(xb, sb_in, j, batch=batch, seq_len=seq_len)
            w_out = sb_out
        else:
            mix = _lru_mixer(xb, lru_in, j, lru_conv_w[j], lru_conv_b[j], lru_w_gate_a[j], lru_b_gate_a[j],
                             lru_w_gate_x[j], lru_b_gate_x[j], lru_lambda[j], batch=batch, seq_len=seq_len)
            w_out = lru_out
        xf, xb = _outproj_ln(mix, w_out, j, xf, ln_gain[i, 0], ln_bias[i, 0], alpha, seq_len=seq_len)
        hid = _proj(xb, ffn_up, i, col0=0, ncols=d_ff, seq_len=seq_len, out_dtype=BF16, cw=ffn_conv_w[i],
                    bias=ffn_conv_b[i], act="silu", gated=True, name="ffn_up")
        y = _proj(hid, ffn_down, i, col0=0, ncols=d, seq_len=seq_len, out_dtype=F32, resid=xf, alpha=alpha,
                  name="ffn_down")
        xf, xb = _layer_norm(y, ln_gain[i, 1], ln_bias[i, 1])
    return xf.reshape(batch, seq_len, d)
```
